```python
import jax, jax.numpy as jnp
from jax import lax
import numpy as np

D_MODEL = 1024
BATCH = 16
SEQ = 2048
DEPTH = 1
DEC_BATCH = 128
DEC_SEQ = 1
PAST_LEN = 8192
PAGE_SIZE = 128

N_HEADS_A = 8
HEAD_DIM_A = D_MODEL // N_HEADS_A
N_KV_A = 4
ROT_A = HEAD_DIM_A // 4
N_IDX_HEADS = 8
IDX_DIM = 64
ROT_IDX = IDX_DIM // 4
TOPK_MAX = 256
N_HEADS_M = 8
Q_LORA = 256
KV_LORA = 256
NOPE_DIM = 128
ROPE_DIM_M = 64
V_DIM_M = D_MODEL // N_HEADS_M
MLA_SCALE = (NOPE_DIM + ROPE_DIM_M) ** -0.5
D_FF = 4 * D_MODEL
ROPE_THETA = 500000.0
NORM_EPS = 1e-6
Q_BLOCK = 128
IN_SIZES = (N_HEADS_A * HEAD_DIM_A, N_KV_A * HEAD_DIM_A, N_KV_A * HEAD_DIM_A,
            N_IDX_HEADS * IDX_DIM, IDX_DIM, N_IDX_HEADS,
            Q_LORA, KV_LORA, ROPE_DIM_M, 2 * D_MODEL)
IN_WIDTH = sum(IN_SIZES)

kernel_name = 'hybrid_dsa_mla_gated_decoder_step'


def rms_norm(x, g):
    xf = x.astype(jnp.float32)
    y = xf * lax.rsqrt(jnp.mean(xf * xf, axis=-1, keepdims=True) + NORM_EPS)
    return (y * g.astype(jnp.float32)).astype(x.dtype)


def rope(x, pos, rot_dim):
    half = rot_dim // 2
    inv = jnp.power(jnp.float32(ROPE_THETA), -jnp.arange(half, dtype=jnp.float32) * (2.0 / rot_dim))
    ang = pos.astype(jnp.float32)[:, None] * inv[None, :]
    cos = jnp.cos(ang)[:, None, :]
    sin = jnp.sin(ang)[:, None, :]
    xf = x.astype(jnp.float32)
    x1, x2, rest = xf[..., :half], xf[..., half:rot_dim], xf[..., rot_dim:]
    out = jnp.concatenate([x1 * cos - x2 * sin, x2 * cos + x1 * sin, rest], axis=-1)
    return out.astype(x.dtype)


def split_points():
    pts, acc = [], 0
    for s in IN_SIZES[:-1]:
        acc += s
        pts.append(acc)
    return pts


def project(x, pos, norm1_g, w_in, a_qn_g, a_kn_g, q_a_norm_g, w_q_b, kv_a_norm_g, m_qn_g, m_qr_g, m_kr_g):
    b, t, _ = x.shape
    h = rms_norm(x, norm1_g)
    z = jnp.einsum('btd,de->bte', h, w_in)
    zq, zk, zv, zqi, zki, ziw, zqa, zkva, zpe, zg = jnp.split(z, split_points(), axis=-1)
    q = rope(rms_norm(zq.reshape(b, t, N_HEADS_A, HEAD_DIM_A), a_qn_g), pos, ROT_A)
    k = rope(rms_norm(zk.reshape(b, t, N_KV_A, HEAD_DIM_A), a_kn_g), pos, ROT_A)
    v = zv.reshape(b, t, N_KV_A, HEAD_DIM_A)
    qi = rope(zqi.reshape(b, t, N_IDX_HEADS, IDX_DIM), pos, ROT_IDX)
    ki = rope(zki[:, :, None, :], pos, ROT_IDX)[:, :, 0, :]
    qm = jnp.einsum('btr,rhe->bthe', rms_norm(zqa, q_a_norm_g), w_q_b)
    q_nope = rms_norm(qm[..., :NOPE_DIM], m_qn_g)
    q_pe = rope(rms_norm(qm[..., NOPE_DIM:], m_qr_g), pos, ROPE_DIM_M)
    ckv = rms_norm(zkva, kv_a_norm_g)
    kpe = rope(rms_norm(zpe, m_kr_g)[:, :, None, :], pos, ROPE_DIM_M)[:, :, 0, :]
    g = jax.nn.sigmoid(zg.astype(jnp.float32)).astype(x.dtype)
    return {'q': q, 'k': k, 'v': v, 'qi': qi, 'ki': ki, 'iw': ziw, 'q_nope': q_nope, 'q_pe': q_pe,
            'ckv': ckv, 'kpe': kpe, 'g_a': g[..., :D_MODEL], 'g_b': g[..., D_MODEL:]}


def mla_keys(c, w_uk, m_kn_g):
    return rms_norm(jnp.einsum('...r,rhd->...hd', c, w_uk), m_kn_g)


def dsa_attend(q, qi, iw, ki, qpos, kpos, fetch_kv, topk):
    tq, n_h, hd = q.shape
    logits = jnp.einsum('thd,sd->tsh', qi, ki).astype(jnp.float32) * (IDX_DIM ** -0.5)
    score = jnp.einsum('tsh,th->ts', jax.nn.relu(logits), iw.astype(jnp.float32) * (N_IDX_HEADS ** -0.5))
    score = jnp.where(kpos[None, :] <= qpos[:, None], score, -jnp.inf)
    _, idx = lax.top_k(score, topk)
    k_sel, v_sel = fetch_kv(idx)
    valid = kpos[idx] <= qpos[:, None]
    qg = q.reshape(tq, N_KV_A, n_h // N_KV_A, hd)
    s = jnp.einsum('tkgd,tnkd->tkgn', qg, k_sel).astype(jnp.float32) * (hd ** -0.5)
    s = jnp.where(valid[:, None, None, :], s, -jnp.inf)
    p = jax.nn.softmax(s, axis=-1).astype(v_sel.dtype)
    o = jnp.einsum('tkgn,tnkd->tkgd', p, v_sel)
    return o.reshape(tq, n_h * hd)


def mla_attend(q_nope, q_pe, k_nope, c, kpe, qpos, kpos, w_uv):
    tq = q_nope.shape[0]
    s = (jnp.einsum('thd,lhd->htl', q_nope, k_nope) + jnp.einsum('thd,ld->htl', q_pe, kpe)).astype(jnp.float32) * MLA_SCALE
    s = jnp.where((kpos[None, :] <= qpos[:, None])[None], s, -jnp.inf)
    p = jax.nn.softmax(s, axis=-1).astype(c.dtype)
    o_lat = jnp.einsum('htl,lr->thr', p, c)
    o = jnp.einsum('thr,rhd->thd', o_lat, w_uv)
    return o.reshape(tq, N_HEADS_M * V_DIM_M)


def merge_and_mlp(x, o_a, o_b, g_a, g_b, w_o, norm2_g, w_up, w_down):
    x1 = x + jnp.einsum('btd,de->bte', g_a * o_a + g_b * o_b, w_o)
    u = jax.nn.relu(jnp.einsum('btd,df->btf', rms_norm(x1, norm2_g), w_up))
    return x1 + jnp.einsum('btf,fd->btd', u * u, w_down)


def setup_inputs(seed: int = 0) -> dict:
    key = jax.random.key(seed)
    ks = jax.random.split(key, 26)
    f32 = jnp.float32
    n_pages = PAST_LEN // PAGE_SIZE
    n_used = DEC_BATCH * n_pages
    n_pool = n_used + n_used // 4 + 1

    def nrm(k, shape, scale=1.0):
        return jax.random.normal(k, shape, f32) * scale

    def gain(k, n):
        return 1.0 + 0.1 * jax.random.normal(k, (n,), f32)

    page_table = jax.random.permutation(ks[7], n_pool)[:n_used].reshape(DEC_BATCH, n_pages).astype(jnp.int32)
    return {
        'x_prompt': nrm(ks[0], (BATCH, SEQ, D_MODEL)),
        'x_sample': nrm(ks[1], (DEC_BATCH, DEC_SEQ, D_MODEL)),
        'cache_k': nrm(ks[2], (n_pool, PAGE_SIZE, N_KV_A, HEAD_DIM_A)),
        'cache_v': nrm(ks[3], (n_pool, PAGE_SIZE, N_KV_A, HEAD_DIM_A)),
        'cache_idx_k': nrm(ks[4], (n_pool, PAGE_SIZE, IDX_DIM)),
        'cache_ckv': nrm(ks[5], (n_pool, PAGE_SIZE, KV_LORA)),
        'cache_kpe': nrm(ks[6], (n_pool, PAGE_SIZE, ROPE_DIM_M)),
        'page_table': page_table,
        'norm1_g': gain(ks[8], D_MODEL),
        'w_in': nrm(ks[9], (D_MODEL, IN_WIDTH), D_MODEL ** -0.5),
        'a_qn_g': gain(ks[10], HEAD_DIM_A),
        'a_kn_g': gain(ks[11], HEAD_DIM_A),
        'q_a_norm_g': gain(ks[12], Q_LORA),
        'w_q_b': nrm(ks[13], (Q_LORA, N_HEADS_M, NOPE_DIM + ROPE_DIM_M), Q_LORA ** -0.5),
        'kv_a_norm_g': gain(ks[14], KV_LORA),
        'w_uk': nrm(ks[15], (KV_LORA, N_HEADS_M, NOPE_DIM), KV_LORA ** -0.5),
        'w_uv': nrm(ks[16], (KV_LORA, N_HEADS_M, V_DIM_M), KV_LORA ** -0.5),
        'm_qn_g': gain(ks[17], NOPE_DIM),
        'm_kn_g': gain(ks[18], NOPE_DIM),
        'm_qr_g': gain(ks[19], ROPE_DIM_M),
        'm_kr_g': gain(ks[20], ROPE_DIM_M),
        'w_o': nrm(ks[21], (D_MODEL, D_MODEL), D_MODEL ** -0.5),
        'norm2_g': gain(ks[22], D_MODEL),
        'w_up': nrm(ks[23], (D_MODEL, D_FF), D_MODEL ** -0.5),
        'w_down': nrm(ks[24], (D_FF, D_MODEL), D_FF ** -0.5),
    }


def reference(x_prompt, x_sample, cache_k, cache_v, cache_idx_k, cache_ckv, cache_kpe, page_table,
              norm1_g, w_in, a_qn_g, a_kn_g, q_a_norm_g, w_q_b, kv_a_norm_g, w_uk, w_uv,
              m_qn_g, m_kn_g, m_qr_g, m_kr_g, w_o, norm2_g, w_up, w_down):
    proj_w = (norm1_g, w_in, a_qn_g, a_kn_g, q_a_norm_g, w_q_b, kv_a_norm_g, m_qn_g, m_qr_g, m_kr_g)
    for _layer in range(DEPTH):
        b_p, s_p, _ = x_prompt.shape
        pos_p = jnp.arange(s_p, dtype=jnp.int32)
        P = project(x_prompt, pos_p, *proj_w)
        kn_p = mla_keys(P['ckv'], w_uk, m_kn_g)
        topk_p = min(TOPK_MAX, s_p // 4)
        n_blk = s_p // Q_BLOCK

        def to_blocks(a):
            return a.reshape((a.shape[0], n_blk, Q_BLOCK) + a.shape[2:]).swapaxes(0, 1)

        def prompt_block(blk):
            q_b, qi_b, iw_b, qn_b, qp_b, qpos_b = blk

            def one_seq(q, qi, iw, qn, qp, ki, k, v, kn, c, kpe):
                oa = dsa_attend(q, qi, iw, ki, qpos_b, pos_p, lambda idx: (k[idx], v[idx]), topk_p)
                ob = mla_attend(qn, qp, kn, c, kpe, qpos_b, pos_p, w_uv)
                return oa, ob

            return jax.vmap(one_seq)(q_b, qi_b, iw_b, qn_b, qp_b, P['ki'], P['k'], P['v'], kn_p, P['ckv'], P['kpe'])

        oa_p, ob_p = lax.map(prompt_block, (to_blocks(P['q']), to_blocks(P['qi']), to_blocks(P['iw']),
                                            to_blocks(P['q_nope']), to_blocks(P['q_pe']),
                                            pos_p.reshape(n_blk, Q_BLOCK)))
        oa_p = oa_p.swapaxes(0, 1).reshape(b_p, s_p, D_MODEL)
        ob_p = ob_p.swapaxes(0, 1).reshape(b_p, s_p, D_MODEL)
        y_prompt = merge_and_mlp(x_prompt, oa_p, ob_p, P['g_a'], P['g_b'], w_o, norm2_g, w_up, w_down)

        t_s = x_sample.shape[1]
        pos_s = PAST_LEN + jnp.arange(t_s, dtype=jnp.int32)
        S = project(x_sample, pos_s, *proj_w)
        n_keys = PAST_LEN + t_s
        kpos_s = jnp.arange(n_keys, dtype=jnp.int32)
        topk_s = min(TOPK_MAX, n_keys // 4)

        def gather_past(pool, pt):
            return pool[pt].reshape((-1,) + pool.shape[2:])

        def sample_seq(args):
            pt, q, qi, iw, ki_new, k_new, v_new, qn, qp, c_new, kpe_new = args
            ki_all = jnp.concatenate([gather_past(cache_idx_k, pt), ki_new], axis=0)

            def fetch(idx):
                past = idx < PAST_LEN
                pidx = jnp.minimum(idx, PAST_LEN - 1)
                phys = pt[pidx // PAGE_SIZE]
                off = pidx % PAGE_SIZE
                nidx = jnp.clip(idx - PAST_LEN, 0, t_s - 1)
                sel = past[..., None, None]
                return (jnp.where(sel, cache_k[phys, off], k_new[nidx]),
                        jnp.where(sel, cache_v[phys, off], v_new[nidx]))

            oa = dsa_attend(q, qi, iw, ki_all, pos_s, kpos_s, fetch, topk_s)
            c_all = jnp.concatenate([gather_past(cache_ckv, pt), c_new], axis=0)
            kpe_all = jnp.concatenate([gather_past(cache_kpe, pt), kpe_new], axis=0)
            ob = mla_attend(qn, qp, mla_keys(c_all, w_uk, m_kn_g), c_all, kpe_all, pos_s, kpos_s, w_uv)
            return oa, ob

        oa_s, ob_s = lax.map(sample_seq, (page_table, S['q'], S['qi'], S['iw'], S['ki'], S['k'], S['v'],
                                          S['q_nope'], S['q_pe'], S['ckv'], S['kpe']))
        y_sample = merge_and_mlp(x_sample, oa_s, ob_s, S['g_a'], S['g_b'], w_o, norm2_g, w_up, w_down)
    return (y_prompt, y_sample, P['k'], P['v'], P['ki'], P['ckv'], P['kpe'], S['k'], S['v'], S['ki'], S['ckv'], S['kpe'])
```

```python
import functools

import jax
import jax.numpy as jnp
from jax import lax
from jax.experimental import pallas as pl
from jax.experimental.pallas import tpu as pltpu

F32 = jnp.float32
BF16 = jnp.bfloat16
I32 = jnp.int32

N_HEADS_A = 8
HEAD_DIM_A = 128
N_KV_A = 4
ROT_A = 32
N_IDX_HEADS = 8
IDX_DIM = 64
ROT_IDX = 16
TOPK_MAX = 256
N_HEADS_M = 8
Q_LORA = 256
KV_LORA = 256
NOPE_DIM = 128
ROPE_DIM_M = 64
V_DIM_M = 128
MLA_SCALE = (NOPE_DIM + ROPE_DIM_M) ** -0.5
ROPE_THETA = 500000.0
NORM_EPS = 1e-6
LANES = 128
NEG_INF = float("-inf")

C_Q, C_K, C_V, C_QI, C_KP, C_IW, C_QA, C_KVA, C_G, C_END = 0, 1024, 1536, 2048, 2560, 2688, 2816, 3072, 3328, 5376

T_A, T_I, T_M, T_KP, N_TAB = 0, 3, 6, 9, 14

VMEM_LIMIT = 56 * 1024 * 1024


def _dot(a, b):
    return jnp.dot(a, b, preferred_element_type=F32)


def _dot_nt(a, b):
    return lax.dot_general(a, b, (((1,), (1,)), ((), ())), preferred_element_type=F32)


def _rms(z, g):
    ms = jnp.mean(z * z, axis=-1, keepdims=True)
    return z * lax.rsqrt(ms + NORM_EPS) * g


def _rope3(y, c, s1, s2, shift):
    return y * c + pltpu.roll(y, shift, 1) * s1 + pltpu.roll(y, LANES - shift, 1) * s2


def _const_spec(shape):
    nd = len(shape)
    return pl.BlockSpec(shape, lambda *_: (0,) * nd, pipeline_mode=pl.Buffered(1))


def _rope_tables(pos):
    pos = pos.astype(F32)
    lane = jnp.arange(LANES)

    def angles(rot):
        half = rot // 2
        inv = jnp.power(jnp.float32(ROPE_THETA), -jnp.arange(half, dtype=F32) * (2.0 / rot))
        ang = pos[:, None] * inv[None, :]
        return jnp.cos(ang), jnp.sin(ang), half

    def tables(rot, period, lanes_sel=None):
        cos, sin, half = angles(rot)
        lp = lane % period
        in_rot = lp < rot
        first = lp < half
        second = in_rot & ~first
        cidx = lp % half
        c = jnp.where(in_rot[None, :], cos[:, cidx], 1.0)
        s1 = jnp.where(second[None, :], sin[:, cidx], 0.0)
        s2 = jnp.where(first[None, :], -sin[:, cidx], 0.0)
        if lanes_sel is not None:
            c = jnp.where(lanes_sel[None, :], c, 1.0)
            s1 = jnp.where(lanes_sel[None, :], s1, 0.0)
            s2 = jnp.where(lanes_sel[None, :], s2, 0.0)
        return c, s1, s2

    ca, s1a, s2a = tables(ROT_A, HEAD_DIM_A)
    ci, s1i, s2i = tables(ROT_IDX, IDX_DIM)
    cm, s1m, s2m = tables(ROPE_DIM_M, ROPE_DIM_M)
    lo = lane < IDX_DIM
    cil, s1il, s2il = tables(ROT_IDX, IDX_DIM, lo)
    cmh, s1mh, s2mh = tables(ROPE_DIM_M, ROPE_DIM_M, ~lo)
    ckp = jnp.where(lo[None, :], cil, cmh)
    return jnp.concatenate([ca, s1a, s2a, ci, s1i, s2i, cm, s1m, s2m, ckp, s1il, s2il, s1mh, s2mh], axis=1)


def _proj_kernel(x_ref, tab_ref, g1_ref, w_ref, gq_ref, gk_ref, gqa_ref, wqb_ref, gkva_ref, wuk_ref,
                 gqn_ref, gkn_ref, gqr_ref, gkp_ref,
                 q_o, kf_o, kb_o, vf_o, vb_o, qi_o, kif_o, kpef_o, kik_o, kpek_o, iw_o,
                 qn_o, qpe_o, cf_o, cb_o, kn_o, g_o):
    def tab(i):
        return tab_ref[:, i * LANES:(i + 1) * LANES]

    x = x_ref[...]
    ms = jnp.mean(x * x, axis=-1, keepdims=True)
    h = (x * lax.rsqrt(ms + NORM_EPS) * g1_ref[...]).astype(BF16)

    lane = lax.broadcasted_iota(I32, (1, LANES), 1)
    lo = lane < IDX_DIM

    def seg_rms64(z, g):
        sq = z * z
        s_lo = jnp.sum(jnp.where(lo, sq, 0.0), axis=-1, keepdims=True)
        s_hi = jnp.sum(jnp.where(lo, 0.0, sq), axis=-1, keepdims=True)
        ms64 = jnp.where(lo, s_lo, s_hi) * (1.0 / ROPE_DIM_M)
        return z * lax.rsqrt(ms64 + NORM_EPS) * g

    z = _dot(h, w_ref[:, C_Q:C_K])
    for c in range(N_HEADS_A):
        y = _rms(z[:, c * LANES:(c + 1) * LANES], gq_ref[...])
        y = _rope3(y, tab(T_A), tab(T_A + 1), tab(T_A + 2), ROT_A // 2)
        q_o[:, c * LANES:(c + 1) * LANES] = y.astype(BF16)
    z = _dot(h, w_ref[:, C_K:C_V])
    for c in range(N_KV_A):
        y = _rms(z[:, c * LANES:(c + 1) * LANES], gk_ref[...])
        y = _rope3(y, tab(T_A), tab(T_A + 1), tab(T_A + 2), ROT_A // 2)
        kf_o[:, c * LANES:(c + 1) * LANES] = y
        kb_o[:, c * LANES:(c + 1) * LANES] = y.astype(BF16)
    z = _dot(h, w_ref[:, C_V:C_QI])
    vf_o[...] = z
    vb_o[...] = z.astype(BF16)
    z = _dot(h, w_ref[:, C_QI:C_KP])
    for c in range(N_IDX_HEADS * IDX_DIM // LANES):
        y = _rope3(z[:, c * LANES:(c + 1) * LANES], tab(T_I), tab(T_I + 1), tab(T_I + 2), ROT_IDX // 2)
        qi_o[:, c * LANES:(c + 1) * LANES] = y.astype(BF16)
    z = _dot(h, w_ref[:, C_KP:C_IW])
    sq = z * z
    s_hi = jnp.sum(jnp.where(lo, 0.0, sq), axis=-1, keepdims=True)
    r_hi = lax.rsqrt(s_hi * (1.0 / ROPE_DIM_M) + NORM_EPS)
    y = jnp.where(lo, z, z * r_hi * gkp_ref[...])
    y = (y * tab(T_KP)
         + pltpu.roll(y, ROT_IDX // 2, 1) * tab(T_KP + 1)
         + pltpu.roll(y, LANES - ROT_IDX // 2, 1) * tab(T_KP + 2)
         + pltpu.roll(y, ROPE_DIM_M // 2, 1) * tab(T_KP + 3)
         + pltpu.roll(y, LANES - ROPE_DIM_M // 2, 1) * tab(T_KP + 4))
    kif_o[...] = y[:, :IDX_DIM]
    kpef_o[...] = y[:, IDX_DIM:]
    ki_lo = jnp.where(lo, y, 0.0)
    kpe_hi = jnp.where(lo, 0.0, y)
    kik_o[:, :LANES] = ki_lo.astype(BF16)
    kik_o[:, LANES:] = pltpu.roll(ki_lo, IDX_DIM, 1).astype(BF16)
    kpek_o[:, :LANES] = pltpu.roll(kpe_hi, IDX_DIM, 1).astype(BF16)
    kpek_o[:, LANES:] = kpe_hi.astype(BF16)
    z = _dot(h, w_ref[:, C_IW:C_QA])
    iw_o[...] = z * (N_IDX_HEADS ** -0.5) * (IDX_DIM ** -0.5)
    z = _dot(h, w_ref[:, C_QA:C_KVA])
    qa = _rms(z, gqa_ref[...]).astype(BF16)
    qm = _dot(qa, wqb_ref[...])
    for c in range(N_HEADS_M):
        y = _rms(qm[:, c * LANES:(c + 1) * LANES], gqn_ref[...])
        qn_o[:, c * LANES:(c + 1) * LANES] = y.astype(BF16)
    base = N_HEADS_M * NOPE_DIM
    for c in range(N_HEADS_M * ROPE_DIM_M // LANES):
        y = seg_rms64(qm[:, base + c * LANES:base + (c + 1) * LANES], gqr_ref[...])
        y = _rope3(y, tab(T_M), tab(T_M + 1), tab(T_M + 2), ROPE_DIM_M // 2)
        qpe_o[:, c * LANES:(c + 1) * LANES] = y.astype(BF16)
    z = _dot(h, w_ref[:, C_KVA:C_G])
    ckv = _rms(z, gkva_ref[...])
    cf_o[...] = ckv
    cb = ckv.astype(BF16)
    cb_o[...] = cb
    knr = _dot(cb, wuk_ref[...])
    for c in range(N_HEADS_M):
        y = _rms(knr[:, c * LANES:(c + 1) * LANES], gkn_ref[...])
        kn_o[:, c * LANES:(c + 1) * LANES] = y.astype(BF16)
    z = _dot(h, w_ref[:, C_G:C_END])
    g_o[...] = 1.0 / (1.0 + jnp.exp(-z))


def _project(x2d, tab, n_pos_tiles, tm, wts):
    n, d = x2d.shape
    grid = (n // tm,)

    def row(width):
        return pl.BlockSpec((tm, width), lambda i: (i, 0))

    out_defs = [
        (1024, BF16), (512, F32), (512, BF16), (512, F32), (512, BF16), (512, BF16),
        (IDX_DIM, F32), (ROPE_DIM_M, F32), (256, BF16), (256, BF16), (LANES, F32),
        (1024, BF16), (512, BF16), (KV_LORA, F32), (KV_LORA, BF16), (1024, BF16), (2048, F32),
    ]
    consts = [wts["g1"], wts["w_in"], wts["gq"], wts["gk"], wts["gqa"], wts["wqb"], wts["gkva"], wts["wuk"],
              wts["gqn"], wts["gkn"], wts["gqr"], wts["gkp"]]
    in_specs = [row(d), pl.BlockSpec((tm, N_TAB * LANES), lambda i: (i % n_pos_tiles, 0))]
    in_specs += [_const_spec(c.shape) for c in consts]
    return pl.pallas_call(
        _proj_kernel,
        grid=grid,
        in_specs=in_specs,
        out_specs=[row(w) for w, _ in out_defs],
        out_shape=[jax.ShapeDtypeStruct((n, w), dt) for w, dt in out_defs],
        compiler_params=pltpu.CompilerParams(dimension_semantics=("parallel",), vmem_limit_bytes=VMEM_LIMIT),
        name="project",
    )(x2d, tab, *consts)


def _topk_mask(score, k, ut):
    r, n = score.shape
    bits = lax.bitcast_convert_type(score, I32)
    key = bits ^ ((bits >> 31) & jnp.int32(0x7FFFFFFF))
    sign = jnp.int32(-2147483648)

    def body(b, t):
        cand_u = t | lax.shift_left(jnp.int32(1), 31 - b)
        cand_s = cand_u ^ sign
        cnt = jnp.sum((key >= cand_s).astype(F32), axis=-1, keepdims=True)
        return jnp.where(cnt >= k, cand_u, t)

    t = lax.fori_loop(0, 32, body, jnp.zeros((r, 1), I32))
    thr = t ^ sign
    gt = key > thr
    eq = key == thr
    need = k - jnp.sum(gt.astype(F32), axis=-1, keepdims=True)
    off = jnp.zeros((r, 1), F32)
    parts = []
    for c in range(n // LANES):
        sl = slice(c * LANES, (c + 1) * LANES)
        eqc = eq[:, sl]
        pc = _dot(jnp.where(eqc, 1.0, 0.0).astype(BF16), ut)
        parts.append(gt[:, sl] | (eqc & ((pc + off) <= need)))
        off = off + pc[:, LANES - 1:LANES]
    return jnp.concatenate(parts, axis=1)


def _dsa_prompt_kernel(qi_ref, iw_ref, kik_ref, q_ref, kb_ref, vb_ref, ut_ref, o_ref, *, topk, qb):
    i = pl.program_id(1)
    s_len = kik_ref.shape[0]
    iw = iw_ref[...]
    score = jnp.zeros((qb, s_len), F32)
    for j in range(N_IDX_HEADS // 2):
        qs = qi_ref[:, j * LANES:(j + 1) * LANES]
        l0 = _dot_nt(qs, kik_ref[:, :LANES])
        l1 = _dot_nt(qs, kik_ref[:, LANES:])
        score = score + jnp.maximum(l0, 0.0) * iw[:, 2 * j:2 * j + 1] + jnp.maximum(l1, 0.0) * iw[:, 2 * j + 1:2 * j + 2]
    qpos = i * qb + lax.broadcasted_iota(I32, (qb, 1), 0)
    kpos = lax.broadcasted_iota(I32, (1, s_len), 1)
    valid = kpos <= qpos
    score = jnp.where(valid, score, NEG_INF)
    mask = _topk_mask(score, topk, ut_ref[...]) & valid
    scale = HEAD_DIM_A ** -0.5
    rep = N_HEADS_A // N_KV_A
    for kv in range(N_KV_A):
        kh = kb_ref[:, kv * LANES:(kv + 1) * LANES]
        vh = vb_ref[:, kv * LANES:(kv + 1) * LANES]
        for g in range(rep):
            hh = kv * rep + g
            s = _dot_nt(q_ref[:, hh * LANES:(hh + 1) * LANES], kh) * scale
            s = jnp.where(mask, s, NEG_INF)
            m = jnp.max(s, axis=-1, keepdims=True)
            p = jnp.exp(s - m)
            l = jnp.sum(p, axis=-1, keepdims=True)
            o_ref[:, hh * LANES:(hh + 1) * LANES] = _dot(p.astype(BF16), vh) / l


def _dsa_prompt(qi, iw, kik, q, kb, vb, ut, b, s_len, topk, qb):
    nq = s_len // qb

    def qblk(w):
        return pl.BlockSpec((qb, w), lambda bi, i: (bi * nq + i, 0))

    def seq(w):
        return pl.BlockSpec((s_len, w), lambda bi, i: (bi, 0))

    return pl.pallas_call(
        functools.partial(_dsa_prompt_kernel, topk=topk, qb=qb),
        grid=(b, nq),
        in_specs=[qblk(512), qblk(LANES), seq(256), qblk(1024), seq(512), seq(512), _const_spec(ut.shape)],
        out_specs=qblk(1024),
        out_shape=jax.ShapeDtypeStruct((b * s_len, 1024), F32),
        compiler_params=pltpu.CompilerParams(dimension_semantics=("parallel", "arbitrary"),
                                             vmem_limit_bytes=VMEM_LIMIT),
        name="dsa_prompt",
    )(qi, iw, kik, q, kb, vb, ut)


def _mla_prompt_kernel(qn_ref, qpe_ref, kn_ref, kpek_ref, cb_ref, wuv_ref, o_ref, *, qb):
    i = pl.program_id(1)
    s_len = kn_ref.shape[0]
    qpos = i * qb + lax.broadcasted_iota(I32, (qb, 1), 0)
    kpos = lax.broadcasted_iota(I32, (1, s_len), 1)
    valid = kpos <= qpos
    for h in range(N_HEADS_M):
        sl = slice(h * LANES, (h + 1) * LANES)
        pe_q = qpe_ref[:, (h // 2) * LANES:(h // 2 + 1) * LANES]
        pe_k = kpek_ref[:, (h % 2) * LANES:(h % 2 + 1) * LANES]
        s = (_dot_nt(qn_ref[:, sl], kn_ref[:, sl]) + _dot_nt(pe_q, pe_k)) * MLA_SCALE
        s = jnp.where(valid, s, NEG_INF)
        m = jnp.max(s, axis=-1, keepdims=True)
        p = jnp.exp(s - m)
        l = jnp.sum(p, axis=-1, keepdims=True)
        o_lat = _dot(p.astype(BF16), cb_ref[...]) / l
        o_ref[:, sl] = _dot(o_lat.astype(BF16), wuv_ref[:, sl])


def _mla_prompt(qn, qpe, kn, kpek, cb, wuv, b, s_len, qb):
    nq = s_len // qb

    def qblk(w):
        return pl.BlockSpec((qb, w), lambda bi, i: (bi * nq + i, 0))

    def seq(w):
        return pl.BlockSpec((s_len, w), lambda bi, i: (bi, 0))

    return pl.pallas_call(
        functools.partial(_mla_prompt_kernel, qb=qb),
        grid=(b, nq),
        in_specs=[qblk(1024), qblk(512), seq(1024), seq(256), seq(KV_LORA), _const_spec(wuv.shape)],
        out_specs=qblk(1024),
        out_shape=jax.ShapeDtypeStruct((b * s_len, 1024), F32),
        compiler_params=pltpu.CompilerParams(dimension_semantics=("parallel", "arbitrary"),
                                             vmem_limit_bytes=VMEM_LIMIT),
        name="mla_prompt",
    )(qn, qpe, kn, kpek, cb, wuv)


def _merge_mlp_kernel(x_ref, oa_ref, ob_ref, g_ref, wo_ref, g2_ref, wup_ref, wdn_ref, y_ref, *, ff_chunk):
    d = x_ref.shape[1]
    m = (g_ref[:, :d] * oa_ref[...] + g_ref[:, d:] * ob_ref[...]).astype(BF16)
    x1 = x_ref[...] + _dot(m, wo_ref[...])
    ms = jnp.mean(x1 * x1, axis=-1, keepdims=True)
    h2 = (x1 * lax.rsqrt(ms + NORM_EPS) * g2_ref[...]).astype(BF16)
    acc = x1
    for c in range(wup_ref.shape[1] // ff_chunk):
        u = jnp.maximum(_dot(h2, wup_ref[:, c * ff_chunk:(c + 1) * ff_chunk]), 0.0)
        acc = acc + _dot((u * u).astype(BF16), wdn_ref[c * ff_chunk:(c + 1) * ff_chunk, :])
    y_ref[...] = acc


def _merge_mlp(x2d, oa, ob, g, wo, g2, wup, wdn, tm):
    n, d = x2d.shape

    def row(w):
        return pl.BlockSpec((tm, w), lambda i: (i, 0))

    return pl.pallas_call(
        functools.partial(_merge_mlp_kernel, ff_chunk=1024),
        grid=(n // tm,),
        in_specs=[row(d), row(d), row(d), row(2 * d), _const_spec(wo.shape), _const_spec(g2.shape),
                  _const_spec(wup.shape), _const_spec(wdn.shape)],
        out_specs=row(d),
        out_shape=jax.ShapeDtypeStruct((n, d), F32),
        compiler_params=pltpu.CompilerParams(dimension_semantics=("parallel",), vmem_limit_bytes=VMEM_LIMIT),
        name="merge_mlp",
    )(x2d, oa, ob, g, wo, g2, wup, wdn)


def _page_copies(pt_ref, cache_ref, buf_ref, sem_ref, seq, slot, n_pages, page):
    return [pltpu.make_async_copy(cache_ref.at[pt_ref[seq * n_pages + p]],
                                  buf_ref.at[slot, pl.ds(p * page, page)],
                                  sem_ref.at[slot]) for p in range(n_pages)]


def _samp_score_kernel(pt_ref, qi_ref, wc_ref, kin_ref, cache_ref, o_ref, buf_ref, sem_ref, *, n_pages, page, chunk):
    s = pl.program_id(0)
    ns = pl.num_programs(0)
    slot = s % 2
    past = n_pages * page

    @pl.when(s == 0)
    def _():
        for cp in _page_copies(pt_ref, cache_ref, buf_ref, sem_ref, 0, 0, n_pages, page):
            cp.start()

    @pl.when(s + 1 < ns)
    def _():
        for cp in _page_copies(pt_ref, cache_ref, buf_ref, sem_ref, s + 1, 1 - slot, n_pages, page):
            cp.start()

    for cp in _page_copies(pt_ref, cache_ref, buf_ref, sem_ref, s, slot, n_pages, page):
        cp.wait()

    qi = qi_ref[0]
    wc = wc_ref[0]

    def score_of(keys):
        l = _dot_nt(qi, keys.astype(BF16))
        return jnp.sum(jnp.maximum(l, 0.0) * wc, axis=0, keepdims=True)

    for c in range(past // chunk):
        o_ref[0, :, c * chunk:(c + 1) * chunk] = score_of(buf_ref[slot, c * chunk:(c + 1) * chunk, :])
    row = lax.broadcasted_iota(I32, (LANES, IDX_DIM), 0)
    tail_keys = jnp.where(row == 0, jnp.broadcast_to(kin_ref[0], (LANES, IDX_DIM)), 0.0)
    lane = lax.broadcasted_iota(I32, (1, LANES), 1)
    o_ref[0, :, past:] = jnp.where(lane == 0, score_of(tail_keys), NEG_INF)


def _samp_scores(pt_flat, qi3, wc3, kin3, cache_idx, n_pages, page):
    ns = qi3.shape[0]
    past = n_pages * page
    chunk = min(1024, past)
    grid_spec = pltpu.PrefetchScalarGridSpec(
        num_scalar_prefetch=1,
        grid=(ns,),
        in_specs=[pl.BlockSpec((1, N_IDX_HEADS, IDX_DIM), lambda s, pt: (s, 0, 0)),
                  pl.BlockSpec((1, N_IDX_HEADS, 1), lambda s, pt: (s, 0, 0)),
                  pl.BlockSpec((1, 1, IDX_DIM), lambda s, pt: (s, 0, 0)),
                  pl.BlockSpec(memory_space=pl.ANY)],
        out_specs=pl.BlockSpec((1, 1, past + LANES), lambda s, pt: (s, 0, 0)),
        scratch_shapes=[pltpu.VMEM((2, past, IDX_DIM), F32), pltpu.SemaphoreType.DMA((2,))],
    )
    return pl.pallas_call(
        functools.partial(_samp_score_kernel, n_pages=n_pages, page=page, chunk=chunk),
        grid_spec=grid_spec,
        out_shape=jax.ShapeDtypeStruct((ns, 1, past + LANES), F32),
        compiler_params=pltpu.CompilerParams(dimension_semantics=("arbitrary",), vmem_limit_bytes=VMEM_LIMIT),
        name="sample_scores",
    )(pt_flat, qi3, wc3, kin3, cache_idx)


def _samp_select_kernel(score_ref, ut_ref, j8_ref, idx_ref, np_ref, rank_ref, *, topk, past, chunk):
    ns = score_ref.shape[0]
    sel = _topk_mask(score_ref[...], topk, ut_ref[...])
    off = jnp.zeros((ns, 1), F32)
    for c in range(past // LANES):
        sl = slice(c * LANES, (c + 1) * LANES)
        sc = sel[:, sl]
        pc = _dot(jnp.where(sc, 1.0, 0.0).astype(BF16), ut_ref[...])
        rank_ref[:, sl] = jnp.where(sc, pc + off, 0.0)
        off = off + pc[:, LANES - 1:LANES]
    np_ref[...] = jnp.broadcast_to(off, np_ref.shape).astype(I32)
    want = (lax.broadcasted_iota(I32, (topk, 1), 0) + 1).astype(F32)

    def per_seq(s, carry):
        acc = jnp.zeros((8, topk), F32)
        for c in range(past // chunk):
            rr = rank_ref[pl.ds(s, 1), c * chunk:(c + 1) * chunk]
            onehot = jnp.where(rr == want, 1.0, 0.0).astype(BF16)
            acc = acc + _dot_nt(j8_ref[:, c * chunk:(c + 1) * chunk], onehot)
        idx_ref[pl.ds(s, 1), :] = (acc[0:1, :] * float(LANES) + acc[1:2, :]).astype(I32)
        return carry

    lax.fori_loop(0, ns, per_seq, 0)


def _samp_select(score2d, ut, j8, topk, past):
    ns, n = score2d.shape
    chunk = min(1024, past)
    return pl.pallas_call(
        functools.partial(_samp_select_kernel, topk=topk, past=past, chunk=chunk),
        grid=(1,),
        in_specs=[pl.BlockSpec((ns, n), lambda i: (0, 0)), pl.BlockSpec(ut.shape, lambda i: (0, 0)),
                  pl.BlockSpec(j8.shape, lambda i: (0, 0))],
        out_specs=[pl.BlockSpec((ns, topk), lambda i: (0, 0)), pl.BlockSpec((ns, LANES), lambda i: (0, 0))],
        out_shape=[jax.ShapeDtypeStruct((ns, topk), I32), jax.ShapeDtypeStruct((ns, LANES), I32)],
        scratch_shapes=[pltpu.VMEM((ns, past), F32)],
        compiler_params=pltpu.CompilerParams(dimension_semantics=("arbitrary",), vmem_limit_bytes=VMEM_LIMIT),
        name="sample_select",
    )(score2d, ut, j8)


def _row_copies(pt_ref, idx_ref, ck_ref, cv_ref, kbuf, vbuf, sem_ref, seq, slot, r, n_pages, page, topk):
    pos = idx_ref[seq * topk + r]
    pg = pt_ref[seq * n_pages + pos // page]
    off = pos % page
    return (pltpu.make_async_copy(ck_ref.at[pg, off], kbuf.at[slot, r], sem_ref.at[0, slot]),
            pltpu.make_async_copy(cv_ref.at[pg, off], vbuf.at[slot, r], sem_ref.at[1, slot]))


def _samp_dsa_kernel(pt_ref, idx_ref, np_ref, q_ref, kn_ref, vn_ref, ck_ref, cv_ref, o_ref, kbuf, vbuf, sem_ref,
                     *, n_pages, page, topk):
    s = pl.program_id(0)
    ns = pl.num_programs(0)
    slot = s % 2

    def start_all(seq, sl):
        def body(r, c):
            ck, cv = _row_copies(pt_ref, idx_ref, ck_ref, cv_ref, kbuf, vbuf, sem_ref, seq, sl, r, n_pages, page, topk)
            ck.start()
            cv.start()
            return c
        lax.fori_loop(0, topk, body, 0)

    @pl.when(s == 0)
    def _():
        start_all(0, 0)

    @pl.when(s + 1 < ns)
    def _():
        start_all(s + 1, 1 - slot)

    def wait_body(r, c):
        ck, cv = _row_copies(pt_ref, idx_ref, ck_ref, cv_ref, kbuf, vbuf, sem_ref, s, slot, r, n_pages, page, topk)
        ck.wait()
        cv.wait()
        return c
    lax.fori_loop(0, topk, wait_body, 0)

    n_past = np_ref[s]
    slot_ok = lax.broadcasted_iota(I32, (1, topk), 1) < n_past
    new_ok = n_past < topk
    scale = HEAD_DIM_A ** -0.5
    rep = N_HEADS_A // N_KV_A
    for kv in range(N_KV_A):
        sl = slice(kv * LANES, (kv + 1) * LANES)
        kh = kbuf[slot, :, kv, :].astype(BF16)
        vh = vbuf[slot, :, kv, :].astype(BF16)
        q2 = q_ref[0, kv * rep:(kv + 1) * rep, :]
        k_new = kn_ref[0, :, sl].astype(BF16).astype(F32)
        v_new = vn_ref[0, :, sl].astype(BF16).astype(F32)
        sc = jnp.where(slot_ok, _dot_nt(q2, kh) * scale, NEG_INF)
        s_new = jnp.sum(q2.astype(F32) * k_new, axis=-1, keepdims=True) * scale
        s_new = jnp.where(new_ok, s_new, NEG_INF)
        m = jnp.maximum(jnp.max(sc, axis=-1, keepdims=True), s_new)
        p = jnp.exp(sc - m)
        p_new = jnp.exp(s_new - m)
        l = jnp.sum(p, axis=-1, keepdims=True) + p_new
        o = (_dot(p.astype(BF16), vh) + p_new.astype(BF16).astype(F32) * v_new) / l
        for g in range(rep):
            hh = kv * rep + g
            o_ref[0, :, hh * LANES:(hh + 1) * LANES] = o[g:g + 1, :]


def _samp_dsa(pt_flat, idx_flat, n_past, q3, kn3, vn3, ck3, cv3, n_pages, page, topk):
    ns = q3.shape[0]
    grid_spec = pltpu.PrefetchScalarGridSpec(
        num_scalar_prefetch=3,
        grid=(ns,),
        in_specs=[pl.BlockSpec((1, N_HEADS_A, HEAD_DIM_A), lambda s, *_: (s, 0, 0)),
                  pl.BlockSpec((1, 1, N_KV_A * HEAD_DIM_A), lambda s, *_: (s, 0, 0)),
                  pl.BlockSpec((1, 1, N_KV_A * HEAD_DIM_A), lambda s, *_: (s, 0, 0)),
                  pl.BlockSpec(memory_space=pl.ANY), pl.BlockSpec(memory_space=pl.ANY)],
        out_specs=pl.BlockSpec((1, 1, N_HEADS_A * HEAD_DIM_A), lambda s, *_: (s, 0, 0)),
        scratch_shapes=[pltpu.VMEM((2, topk, N_KV_A, HEAD_DIM_A), F32),
                        pltpu.VMEM((2, topk, N_KV_A, HEAD_DIM_A), F32),
                        pltpu.SemaphoreType.DMA((2, 2))],
    )
    return pl.pallas_call(
        functools.partial(_samp_dsa_kernel, n_pages=n_pages, page=page, topk=topk),
        grid_spec=grid_spec,
        out_shape=jax.ShapeDtypeStruct((ns, 1, N_HEADS_A * HEAD_DIM_A), F32),
        compiler_params=pltpu.CompilerParams(dimension_semantics=("arbitrary",), vmem_limit_bytes=VMEM_LIMIT),
        name="sample_dsa",
    )(pt_flat, idx_flat, n_past, q3, kn3, vn3, ck3, cv3)


def _samp_mla_kernel(pt_ref, qn_ref, qcol_ref, gcol_ref, qpe_ref, cn_ref, knn_ref, kpn_ref, wukt_ref, wuv_ref,
                     cc_ref, cp_ref, o_ref, cbuf, pbuf, s_ref, sem_ref, *, n_pages, page, chunk):
    s = pl.program_id(0)
    ns = pl.num_programs(0)
    slot = s % 2
    past = n_pages * page
    n_chunks = past // chunk

    def copies(seq, sl):
        return (_page_copies(pt_ref, cc_ref, cbuf, sem_ref.at[0], seq, sl, n_pages, page)
                + _page_copies(pt_ref, cp_ref, pbuf, sem_ref.at[1], seq, sl, n_pages, page))

    @pl.when(s == 0)
    def _():
        for cp in copies(0, 0):
            cp.start()

    @pl.when(s + 1 < ns)
    def _():
        for cp in copies(s + 1, 1 - slot):
            cp.start()

    for cp in copies(s, slot):
        cp.wait()

    qg = qcol_ref[0].astype(F32) * gcol_ref[...]
    qpe = qpe_ref[0]

    def score_chunk(c, carry):
        start = pl.multiple_of(c * chunk, chunk)
        cb = cbuf[slot, pl.ds(start, chunk), :].astype(BF16)
        knt = _dot_nt(wukt_ref[...], cb)
        s_pe = _dot_nt(qpe, pbuf[slot, pl.ds(start, chunk), :].astype(BF16))
        for h in range(N_HEADS_M):
            blk = knt[h * LANES:(h + 1) * LANES, :]
            ssq = jnp.sum(blk * blk, axis=0, keepdims=True)
            dq = jnp.sum(blk * qg[h * LANES:(h + 1) * LANES, :], axis=0, keepdims=True)
            s_h = dq * lax.rsqrt(ssq * (1.0 / NOPE_DIM) + NORM_EPS) + s_pe[h:h + 1, :]
            s_ref[h:h + 1, pl.ds(start, chunk)] = s_h * MLA_SCALE
        return carry

    lax.fori_loop(0, n_chunks, score_chunk, 0)
    qn = qn_ref[0].astype(F32)
    s_new = (jnp.sum(qn * knn_ref[0].astype(F32), axis=-1, keepdims=True)
             + jnp.sum(qpe.astype(F32) * kpn_ref[0].astype(BF16).astype(F32), axis=-1, keepdims=True)) * MLA_SCALE
    lane = lax.broadcasted_iota(I32, (1, LANES), 1)
    s_ref[:, past:] = jnp.where(lane == 0, s_new, NEG_INF)

    sc = s_ref[...]
    m = jnp.max(sc, axis=-1, keepdims=True)
    p = jnp.exp(sc - m)
    l = jnp.sum(p, axis=-1, keepdims=True)
    s_ref[...] = p
    p_new = jnp.exp(s_new - m)

    def pv_chunk(c, acc):
        start = pl.multiple_of(c * chunk, chunk)
        cb = cbuf[slot, pl.ds(start, chunk), :].astype(BF16)
        return acc + _dot(s_ref[:, pl.ds(start, chunk)].astype(BF16), cb)

    o_lat = lax.fori_loop(0, n_chunks, pv_chunk, jnp.zeros((N_HEADS_M, KV_LORA), F32))
    o_lat = (o_lat + p_new.astype(BF16).astype(F32) * cn_ref[0].astype(BF16).astype(F32)) / l
    o8 = _dot(o_lat.astype(BF16), wuv_ref[...])
    for h in range(N_HEADS_M):
        o_ref[0, :, h * LANES:(h + 1) * LANES] = o8[h:h + 1, h * LANES:(h + 1) * LANES]


def _samp_mla(pt_flat, qn3, qcol, gcol, qpe3, cn3, knn3, kpn3, wukt, wuv, cache_ckv, cache_kpe, n_pages, page):
    ns = qn3.shape[0]
    past = n_pages * page
    chunk = min(1024, past)

    def per_seq(shape):
        return pl.BlockSpec((1,) + shape, lambda s, pt: (s, 0, 0))

    def const(shape):
        nd = len(shape)
        return pl.BlockSpec(shape, lambda s, pt: (0,) * nd, pipeline_mode=pl.Buffered(1))

    grid_spec = pltpu.PrefetchScalarGridSpec(
        num_scalar_prefetch=1,
        grid=(ns,),
        in_specs=[per_seq((N_HEADS_M, NOPE_DIM)), per_seq((N_HEADS_M * NOPE_DIM, 1)), const(gcol.shape),
                  per_seq((N_HEADS_M, ROPE_DIM_M)), per_seq((1, KV_LORA)), per_seq((N_HEADS_M, NOPE_DIM)),
                  per_seq((1, ROPE_DIM_M)), const(wukt.shape), const(wuv.shape),
                  pl.BlockSpec(memory_space=pl.ANY), pl.BlockSpec(memory_space=pl.ANY)],
        out_specs=per_seq((1, N_HEADS_M * V_DIM_M)),
        scratch_shapes=[pltpu.VMEM((2, past, KV_LORA), F32), pltpu.VMEM((2, past, ROPE_DIM_M), F32),
                        pltpu.VMEM((N_HEADS_M, past + LANES), F32), pltpu.SemaphoreType.DMA((2, 2))],
    )
    return pl.pallas_call(
        functools.partial(_samp_mla_kernel, n_pages=n_pages, page=page, chunk=chunk),
        grid_spec=grid_spec,
        out_shape=jax.ShapeDtypeStruct((ns, 1, N_HEADS_M * V_DIM_M), F32),
        compiler_params=pltpu.CompilerParams(dimension_semantics=("arbitrary",), vmem_limit_bytes=VMEM_LIMIT),
        name="sample_mla",
    )(pt_flat, qn3, qcol, gcol, qpe3, cn3, knn3, kpn3, wukt, wuv, cache_ckv, cache_kpe)


def _prep_weights(norm1_g, w_in, a_qn_g, a_kn_g, q_a_norm_g, w_q_b, kv_a_norm_g, w_uk, m_qn_g, m_kn_g, m_qr_g, m_kr_g):
    d = w_in.shape[0]
    sizes = (1024, 512, 512, 512, IDX_DIM, N_IDX_HEADS, Q_LORA, KV_LORA, ROPE_DIM_M, 2 * d)
    offs = [0]
    for sz in sizes:
        offs.append(offs[-1] + sz)
    wq, wk, wv, wqi, wki, wiw, wqa, wkva, wpe, wg = [w_in[:, offs[i]:offs[i + 1]] for i in range(len(sizes))]
    w_perm = jnp.concatenate(
        [wq, wk, wv, wqi, wki, wpe, wiw, jnp.zeros((d, LANES - N_IDX_HEADS), w_in.dtype), wqa, wkva, wg],
        axis=1).astype(BF16)
    wqb = jnp.concatenate([w_q_b[:, :, :NOPE_DIM].reshape(Q_LORA, -1), w_q_b[:, :, NOPE_DIM:].reshape(Q_LORA, -1)],
                          axis=1).astype(BF16)

    def rowv(v):
        return v.astype(F32).reshape(1, -1)

    return {
        "g1": rowv(norm1_g), "w_in": w_perm, "gq": rowv(a_qn_g), "gk": rowv(a_kn_g), "gqa": rowv(q_a_norm_g),
        "wqb": wqb, "gkva": rowv(kv_a_norm_g), "wuk": w_uk.reshape(KV_LORA, -1).astype(BF16),
        "gqn": rowv(m_qn_g), "gkn": rowv(m_kn_g), "gqr": rowv(jnp.tile(m_qr_g, 2)),
        "gkp": rowv(jnp.concatenate([jnp.ones((IDX_DIM,), F32), m_kr_g.astype(F32)])),
    }


def kernel(x_prompt, x_sample, cache_k, cache_v, cache_idx_k, cache_ckv, cache_kpe, page_table,
           norm1_g, w_in, a_qn_g, a_kn_g, q_a_norm_g, w_q_b, kv_a_norm_g, w_uk, w_uv,
           m_qn_g, m_kn_g, m_qr_g, m_kr_g, w_o, norm2_g, w_up, w_down):
    b, s_len, d = x_prompt.shape
    ns, t_s, _ = x_sample.shape
    n_pool, page = cache_k.shape[0], cache_k.shape[1]
    n_pages = page_table.shape[1]
    past = n_pages * page
    assert d == N_HEADS_A * HEAD_DIM_A and t_s == 1
    assert w_in.shape[1] == C_END - (LANES - N_IDX_HEADS)

    wts = _prep_weights(norm1_g, w_in, a_qn_g, a_kn_g, q_a_norm_g, w_q_b, kv_a_norm_g, w_uk,
                        m_qn_g, m_kn_g, m_qr_g, m_kr_g)
    wuv = w_uv.reshape(KV_LORA, -1).astype(BF16)
    wo = w_o.astype(BF16)
    wup = w_up.astype(BF16)
    wdn = w_down.astype(BF16)
    g2 = norm2_g.astype(F32).reshape(1, -1)
    ut = (jnp.arange(LANES)[:, None] <= jnp.arange(LANES)[None, :]).astype(BF16)

    tm = min(256, s_len)
    qb = min(128, s_len)
    tab_p = _rope_tables(jnp.arange(s_len))
    xp = x_prompt.reshape(b * s_len, d)
    (q, kf, kb, vf, vb, qi, kif, kpef, kik, kpek, iw, qn, qpe, cf, cb, kn, g) = _project(xp, tab_p, s_len // tm, tm, wts)
    topk_p = min(TOPK_MAX, s_len // 4)
    oa = _dsa_prompt(qi, iw, kik, q, kb, vb, ut, b, s_len, topk_p, qb)
    ob = _mla_prompt(qn, qpe, kn, kpek, cb, wuv, b, s_len, qb)
    y_prompt = _merge_mlp(xp, oa, ob, g, wo, g2, wup, wdn, tm).reshape(b, s_len, d)

    tab_s = _rope_tables(jnp.full((ns,), past, I32))
    xs = x_sample.reshape(ns, d)
    (sq, skf, _skb, svf, _svb, sqi, skif, skpef, _skik, _skpek, siw, sqn, sqpe, scf, _scb, skn, sg) = _project(
        xs, tab_s, 1, ns, wts)
    topk_s = min(TOPK_MAX, (past + 1) // 4)
    pt_flat = page_table.reshape(-1).astype(I32)
    score = _samp_scores(pt_flat, sqi.reshape(ns, N_IDX_HEADS, IDX_DIM), siw[:, :N_IDX_HEADS].reshape(ns, N_IDX_HEADS, 1),
                         skif.reshape(ns, 1, IDX_DIM), cache_idx_k, n_pages, page)
    pos = jnp.arange(past)
    j8 = jnp.zeros((8, past), F32).at[0].set(pos // LANES).at[1].set(pos % LANES).astype(BF16)
    idx, n_past = _samp_select(score.reshape(ns, past + LANES), ut, j8, topk_s, past)
    oa_s = _samp_dsa(pt_flat, idx.reshape(-1), n_past[:, 0], sq.reshape(ns, N_HEADS_A, HEAD_DIM_A),
                     skf.reshape(ns, 1, -1), svf.reshape(ns, 1, -1),
                     cache_k, cache_v, n_pages, page, topk_s)
    gcol = jnp.tile(m_kn_g.astype(F32), N_HEADS_M).reshape(-1, 1)
    ob_s = _samp_mla(pt_flat, sqn.reshape(ns, N_HEADS_M, NOPE_DIM), sqn.reshape(ns, N_HEADS_M * NOPE_DIM, 1), gcol,
                     sqpe.reshape(ns, N_HEADS_M, ROPE_DIM_M), scf.reshape(ns, 1, KV_LORA),
                     skn.reshape(ns, N_HEADS_M, NOPE_DIM), skpef.reshape(ns, 1, ROPE_DIM_M),
                     w_uk.reshape(KV_LORA, -1).T.astype(BF16), wuv, cache_ckv, cache_kpe, n_pages, page)
    y_sample = _merge_mlp(xs, oa_s.reshape(ns, d), ob_s.reshape(ns, d), sg, wo, g2, wup, wdn, ns).reshape(ns, 1, d)

    return (y_prompt, y_sample,
            kf.reshape(b, s_len, N_KV_A, HEAD_DIM_A), vf.reshape(b, s_len, N_KV_A, HEAD_DIM_A),
            kif.reshape(b, s_len, IDX_DIM), cf.reshape(b, s_len, KV_LORA), kpef.reshape(b, s_len, ROPE_DIM_M),
            skf.reshape(ns, 1, N_KV_A, HEAD_DIM_A), svf.reshape(ns, 1, N_KV_A, HEAD_DIM_A),
            skif.reshape(ns, 1, IDX_DIM), scf.reshape(ns, 1, KV_LORA), skpef.reshape(ns, 1, ROPE_DIM_M))
```

```python
import functools

import jax
import jax.numpy as jnp
from jax import lax
from jax.experimental import pallas as pl
from jax.experimental.pallas import tpu as pltpu

F32 = jnp.float32
BF16 = jnp.bfloat16
I32 = jnp.int32

N_HEADS_A = 8
HEAD_DIM_A = 128
N_KV_A = 4
ROT_A = 32
N_IDX_HEADS = 8
IDX_DIM = 64
ROT_IDX = 16
TOPK_MAX = 256
N_HEADS_M = 8
Q_LORA = 256
KV_LORA = 256
NOPE_DIM = 128
ROPE_DIM_M = 64
V_DIM_M = 128
DSA_SCALE = HEAD_DIM_A ** -0.5
MLA_SCALE = (NOPE_DIM + ROPE_DIM_M) ** -0.5
ROPE_THETA = 500000.0
NORM_EPS = 1e-6
LANES = 128
NEG_INF = float("-inf")

C_Q, C_K, C_V, C_QI, C_KP, C_IW, C_QA, C_KVA, C_G, C_END = 0, 1024, 1536, 2048, 2560, 2688, 2816, 3072, 3328, 5376

T_A, T_I, T_M, T_KP, N_TAB = 0, 3, 6, 9, 14

VMEM_LIMIT = 56 * 1024 * 1024
PROMPT_Q_BLOCK = 256


def _dot(a, b):
    return jnp.dot(a, b, preferred_element_type=F32)


def _dot_nt(a, b):
    return lax.dot_general(a, b, (((1,), (1,)), ((), ())), preferred_element_type=F32)


def _rms(z, g):
    ms = jnp.mean(z * z, axis=-1, keepdims=True)
    return z * lax.rsqrt(ms + NORM_EPS) * g


def _rope3(y, c, s1, s2, shift):
    return y * c + pltpu.roll(y, shift, 1) * s1 + pltpu.roll(y, LANES - shift, 1) * s2


def _const_spec(shape):
    nd = len(shape)
    return pl.BlockSpec(shape, lambda *_: (0,) * nd, pipeline_mode=pl.Buffered(1))


def _rope_tables(pos):
    pos = pos.astype(F32)
    lane = jnp.arange(LANES)

    def tables(rot, period, lanes_sel=None):
        half = rot // 2
        inv = jnp.power(jnp.float32(ROPE_THETA), -jnp.arange(half, dtype=F32) * (2.0 / rot))
        ang = pos[:, None] * inv[None, :]
        cos, sin = jnp.cos(ang), jnp.sin(ang)
        lp = lane % period
        in_rot = lp < rot
        first = lp < half
        second = in_rot & ~first
        cidx = lp % half
        c = jnp.where(in_rot[None, :], cos[:, cidx], 1.0)
        s1 = jnp.where(second[None, :], sin[:, cidx], 0.0)
        s2 = jnp.where(first[None, :], -sin[:, cidx], 0.0)
        if lanes_sel is not None:
            c = jnp.where(lanes_sel[None, :], c, 1.0)
            s1 = jnp.where(lanes_sel[None, :], s1, 0.0)
            s2 = jnp.where(lanes_sel[None, :], s2, 0.0)
        return c, s1, s2

    ca, s1a, s2a = tables(ROT_A, HEAD_DIM_A)
    ci, s1i, s2i = tables(ROT_IDX, IDX_DIM)
    cm, s1m, s2m = tables(ROPE_DIM_M, ROPE_DIM_M)
    lo = lane < IDX_DIM
    cil, s1il, s2il = tables(ROT_IDX, IDX_DIM, lo)
    cmh, s1mh, s2mh = tables(ROPE_DIM_M, ROPE_DIM_M, ~lo)
    ckp = jnp.where(lo[None, :], cil, cmh)
    return jnp.concatenate([ca, s1a, s2a, ci, s1i, s2i, cm, s1m, s2m, ckp, s1il, s2il, s1mh, s2mh], axis=1)


def _proj_kernel(x_ref, tab_ref, g1_ref, w_ref, gq_ref, gk_ref, gqa_ref, wqb_ref, gkva_ref, wuk_ref, wuv_ref,
                 gqn_ref, gkn_ref, gqr_ref, gkp_ref,
                 q_o, kf_o, kb_o, vf_o, vb_o, qi_o, kif_o, kpef_o, kik_o, kpek_o, iw_o,
                 qn_o, qpe_o, cf_o, kn_o, vm_o, g_o):
    def tab(i):
        return tab_ref[:, i * LANES:(i + 1) * LANES]

    x = x_ref[...]
    ms = jnp.mean(x * x, axis=-1, keepdims=True)
    h = (x * lax.rsqrt(ms + NORM_EPS) * g1_ref[...]).astype(BF16)

    lane = lax.broadcasted_iota(I32, (1, LANES), 1)
    lo = lane < IDX_DIM

    def seg_rms64(z, g):
        sq = z * z
        s_lo = jnp.sum(jnp.where(lo, sq, 0.0), axis=-1, keepdims=True)
        s_hi = jnp.sum(jnp.where(lo, 0.0, sq), axis=-1, keepdims=True)
        ms64 = jnp.where(lo, s_lo, s_hi) * (1.0 / ROPE_DIM_M)
        return z * lax.rsqrt(ms64 + NORM_EPS) * g

    z = _dot(h, w_ref[:, C_Q:C_K])
    for c in range(N_HEADS_A):
        y = _rms(z[:, c * LANES:(c + 1) * LANES], gq_ref[...])
        y = _rope3(y, tab(T_A), tab(T_A + 1), tab(T_A + 2), ROT_A // 2)
        q_o[:, c * LANES:(c + 1) * LANES] = (y * DSA_SCALE).astype(BF16)
    z = _dot(h, w_ref[:, C_K:C_V])
    for c in range(N_KV_A):
        y = _rms(z[:, c * LANES:(c + 1) * LANES], gk_ref[...])
        y = _rope3(y, tab(T_A), tab(T_A + 1), tab(T_A + 2), ROT_A // 2)
        kf_o[:, c, :] = y
        kb_o[:, c * LANES:(c + 1) * LANES] = y.astype(BF16)
    z = _dot(h, w_ref[:, C_V:C_QI])
    for c in range(N_KV_A):
        vf_o[:, c, :] = z[:, c * LANES:(c + 1) * LANES]
    vb_o[...] = z.astype(BF16)
    z = _dot(h, w_ref[:, C_QI:C_KP])
    for c in range(N_IDX_HEADS * IDX_DIM // LANES):
        y = _rope3(z[:, c * LANES:(c + 1) * LANES], tab(T_I), tab(T_I + 1), tab(T_I + 2), ROT_IDX // 2)
        qi_o[:, c * LANES:(c + 1) * LANES] = y.astype(BF16)
    z = _dot(h, w_ref[:, C_KP:C_IW])
    sq = z * z
    s_hi = jnp.sum(jnp.where(lo, 0.0, sq), axis=-1, keepdims=True)
    r_hi = lax.rsqrt(s_hi * (1.0 / ROPE_DIM_M) + NORM_EPS)
    y = jnp.where(lo, z, z * r_hi * gkp_ref[...])
    y = (y * tab(T_KP)
         + pltpu.roll(y, ROT_IDX // 2, 1) * tab(T_KP + 1)
         + pltpu.roll(y, LANES - ROT_IDX // 2, 1) * tab(T_KP + 2)
         + pltpu.roll(y, ROPE_DIM_M // 2, 1) * tab(T_KP + 3)
         + pltpu.roll(y, LANES - ROPE_DIM_M // 2, 1) * tab(T_KP + 4))
    kif_o[...] = y[:, :IDX_DIM]
    kpef_o[...] = y[:, IDX_DIM:]
    ki_lo = jnp.where(lo, y, 0.0)
    kpe_hi = jnp.where(lo, 0.0, y)
    kik_o[:, :LANES] = ki_lo.astype(BF16)
    kik_o[:, LANES:] = pltpu.roll(ki_lo, IDX_DIM, 1).astype(BF16)
    kpek_o[:, :LANES] = pltpu.roll(kpe_hi, IDX_DIM, 1).astype(BF16)
    kpek_o[:, LANES:] = kpe_hi.astype(BF16)
    z = _dot(h, w_ref[:, C_IW:C_QA])
    iw_o[...] = z * (N_IDX_HEADS ** -0.5) * (IDX_DIM ** -0.5)
    z = _dot(h, w_ref[:, C_QA:C_KVA])
    qa = _rms(z, gqa_ref[...]).astype(BF16)
    qm = _dot(qa, wqb_ref[...])
    for c in range(N_HEADS_M):
        y = _rms(qm[:, c * LANES:(c + 1) * LANES], gqn_ref[...])
        qn_o[:, c * LANES:(c + 1) * LANES] = (y * MLA_SCALE).astype(BF16)
    base = N_HEADS_M * NOPE_DIM
    for c in range(N_HEADS_M * ROPE_DIM_M // LANES):
        y = seg_rms64(qm[:, base + c * LANES:base + (c + 1) * LANES], gqr_ref[...])
        y = _rope3(y, tab(T_M), tab(T_M + 1), tab(T_M + 2), ROPE_DIM_M // 2)
        qpe_o[:, c * LANES:(c + 1) * LANES] = (y * MLA_SCALE).astype(BF16)
    z = _dot(h, w_ref[:, C_KVA:C_G])
    ckv = _rms(z, gkva_ref[...])
    cf_o[...] = ckv
    cb = ckv.astype(BF16)
    knr = _dot(cb, wuk_ref[...])
    for c in range(N_HEADS_M):
        y = _rms(knr[:, c * LANES:(c + 1) * LANES], gkn_ref[...])
        kn_o[:, c * LANES:(c + 1) * LANES] = y.astype(BF16)
    vm_o[...] = _dot(cb, wuv_ref[...]).astype(BF16)
    z = _dot(h, w_ref[:, C_G:C_END])
    g_o[...] = 1.0 / (1.0 + jnp.exp(-z))


def _project(x2d, tab, n_pos_tiles, tm, wts):
    n, d = x2d.shape
    grid = (n // tm,)

    def row(width):
        return pl.BlockSpec((tm, width), lambda i: (i, 0))

    kv3 = pl.BlockSpec((tm, N_KV_A, HEAD_DIM_A), lambda i: (i, 0, 0))
    kv3_shape = jax.ShapeDtypeStruct((n, N_KV_A, HEAD_DIM_A), F32)
    out_defs = [
        (1024, BF16), None, (512, BF16), None, (512, BF16), (512, BF16),
        (IDX_DIM, F32), (ROPE_DIM_M, F32), (256, BF16), (256, BF16), (LANES, F32),
        (1024, BF16), (512, BF16), (KV_LORA, F32), (1024, BF16), (1024, BF16), (2048, F32),
    ]
    consts = [wts["g1"], wts["w_in"], wts["gq"], wts["gk"], wts["gqa"], wts["wqb"], wts["gkva"], wts["wuk"],
              wts["wuv"], wts["gqn"], wts["gkn"], wts["gqr"], wts["gkp"]]
    in_specs = [row(d), pl.BlockSpec((tm, N_TAB * LANES), lambda i: (i % n_pos_tiles, 0))]
    in_specs += [_const_spec(c.shape) for c in consts]
    return pl.pallas_call(
        _proj_kernel,
        grid=grid,
        in_specs=in_specs,
        out_specs=[kv3 if od is None else row(od[0]) for od in out_defs],
        out_shape=[kv3_shape if od is None else jax.ShapeDtypeStruct((n, od[0]), od[1]) for od in out_defs],
        compiler_params=pltpu.CompilerParams(dimension_semantics=("parallel",), vmem_limit_bytes=VMEM_LIMIT),
        name="project",
    )(x2d, tab, *consts)


def _topk_mask(score, k, ut):
    r, n = score.shape
    bits = lax.bitcast_convert_type(score, I32)
    key = bits ^ ((bits >> 31) & jnp.int32(0x7FFFFFFF))
    sign = jnp.int32(-2147483648)

    def body(b, t):
        cand_u = t | lax.shift_left(jnp.int32(1), 31 - b)
        cand_s = cand_u ^ sign
        cnt = jnp.sum((key >= cand_s).astype(F32), axis=-1, keepdims=True)
        return jnp.where(cnt >= k, cand_u, t)

    t = lax.fori_loop(0, 32, body, jnp.zeros((r, 1), I32))
    thr = t ^ sign
    gt = key > thr
    eq = key == thr
    need = k - jnp.sum(gt.astype(F32), axis=-1, keepdims=True)
    off = jnp.zeros((r, 1), F32)
    parts = []
    for c in range(n // LANES):
        sl = slice(c * LANES, (c + 1) * LANES)
        eqc = eq[:, sl]
        pc = _dot(jnp.where(eqc, 1.0, 0.0).astype(BF16), ut)
        parts.append(gt[:, sl] | (eqc & ((pc + off) <= need)))
        off = off + pc[:, LANES - 1:LANES]
    return jnp.concatenate(parts, axis=1)


def _causal_groups(i, nq, qb, body):
    for g in range(nq):
        pl.when(i == g)(functools.partial(body, (g + 1) * qb))


def _dsa_prompt_kernel(qi_ref, iw_ref, kik_ref, q_ref, kb_ref, vb_ref, ut_ref, o_ref, *, topk, qb, nq):
    i = pl.program_id(1)

    def body(n_keys):
        qpos = i * qb + lax.broadcasted_iota(I32, (qb, 1), 0)
        kpos = lax.broadcasted_iota(I32, (1, n_keys), 1)
        valid = kpos <= qpos
        if n_keys <= topk:
            mask = valid
        else:
            iw = iw_ref[...]
            score = jnp.zeros((qb, n_keys), F32)
            for j in range(N_IDX_HEADS // 2):
                qs = qi_ref[:, j * LANES:(j + 1) * LANES]
                l0 = _dot_nt(qs, kik_ref[:n_keys, :LANES])
                l1 = _dot_nt(qs, kik_ref[:n_keys, LANES:])
                score = (score + jnp.maximum(l0, 0.0) * iw[:, 2 * j:2 * j + 1]
                         + jnp.maximum(l1, 0.0) * iw[:, 2 * j + 1:2 * j + 2])
            score = jnp.where(valid, score, NEG_INF)
            mask = _topk_mask(score, topk, ut_ref[...]) & valid
        rep = N_HEADS_A // N_KV_A
        for kv in range(N_KV_A):
            kh = kb_ref[:n_keys, kv * LANES:(kv + 1) * LANES]
            vh = vb_ref[:n_keys, kv * LANES:(kv + 1) * LANES]
            for g in range(rep):
                hh = kv * rep + g
                s = jnp.where(mask, _dot_nt(q_ref[:, hh * LANES:(hh + 1) * LANES], kh), NEG_INF)
                m = jnp.max(s, axis=-1, keepdims=True)
                p = jnp.exp(s - m)
                l = jnp.sum(p, axis=-1, keepdims=True)
                o_ref[:, hh * LANES:(hh + 1) * LANES] = _dot(p.astype(BF16), vh) / l

    _causal_groups(i, nq, qb, body)


def _dsa_prompt(qi, iw, kik, q, kb, vb, ut, b, s_len, topk, qb):
    nq = s_len // qb

    def qblk(w):
        return pl.BlockSpec((qb, w), lambda bi, i: (bi * nq + i, 0))

    def seq(w):
        return pl.BlockSpec((s_len, w), lambda bi, i: (bi, 0))

    return pl.pallas_call(
        functools.partial(_dsa_prompt_kernel, topk=topk, qb=qb, nq=nq),
        grid=(b, nq),
        in_specs=[qblk(512), qblk(LANES), seq(256), qblk(1024), seq(512), seq(512), _const_spec(ut.shape)],
        out_specs=qblk(1024),
        out_shape=jax.ShapeDtypeStruct((b * s_len, 1024), F32),
        compiler_params=pltpu.CompilerParams(dimension_semantics=("parallel", "arbitrary"),
                                             vmem_limit_bytes=VMEM_LIMIT),
        name="dsa_prompt",
    )(qi, iw, kik, q, kb, vb, ut)


def _mla_prompt_kernel(qn_ref, qpe_ref, kn_ref, kpek_ref, vm_ref, o_ref, *, qb, nq):
    i = pl.program_id(1)

    def body(n_keys):
        qpos = i * qb + lax.broadcasted_iota(I32, (qb, 1), 0)
        kpos = lax.broadcasted_iota(I32, (1, n_keys), 1)
        valid = kpos <= qpos
        for h in range(N_HEADS_M):
            sl = slice(h * LANES, (h + 1) * LANES)
            qcat = jnp.concatenate([qn_ref[:, sl], qpe_ref[:, (h // 2) * LANES:(h // 2 + 1) * LANES]], axis=1)
            kcat = jnp.concatenate([kn_ref[:n_keys, sl], kpek_ref[:n_keys, (h % 2) * LANES:(h % 2 + 1) * LANES]], axis=1)
            s = jnp.where(valid, _dot_nt(qcat, kcat), NEG_INF)
            m = jnp.max(s, axis=-1, keepdims=True)
            p = jnp.exp(s - m)
            l = jnp.sum(p, axis=-1, keepdims=True)
            o_ref[:, sl] = _dot(p.astype(BF16), vm_ref[:n_keys, sl]) / l

    _causal_groups(i, nq, qb, body)


def _mla_prompt(qn, qpe, kn, kpek, vm, b, s_len, qb):
    nq = s_len // qb

    def qblk(w):
        return pl.BlockSpec((qb, w), lambda bi, i: (bi * nq + i, 0))

    def seq(w):
        return pl.BlockSpec((s_len, w), lambda bi, i: (bi, 0))

    return pl.pallas_call(
        functools.partial(_mla_prompt_kernel, qb=qb, nq=nq),
        grid=(b, nq),
        in_specs=[qblk(1024), qblk(512), seq(1024), seq(256), seq(1024)],
        out_specs=qblk(1024),
        out_shape=jax.ShapeDtypeStruct((b * s_len, 1024), F32),
        compiler_params=pltpu.CompilerParams(dimension_semantics=("parallel", "arbitrary"),
                                             vmem_limit_bytes=VMEM_LIMIT),
        name="mla_prompt",
    )(qn, qpe, kn, kpek, vm)


def _merge_mlp_kernel(x_ref, oa_ref, ob_ref, g_ref, wo_ref, g2_ref, wup_ref, wdn_ref, y_ref, *, ff_chunk):
    d = x_ref.shape[1]
    m = (g_ref[:, :d] * oa_ref[...] + g_ref[:, d:] * ob_ref[...]).astype(BF16)
    x1 = x_ref[...] + _dot(m, wo_ref[...])
    ms = jnp.mean(x1 * x1, axis=-1, keepdims=True)
    h2 = (x1 * lax.rsqrt(ms + NORM_EPS) * g2_ref[...]).astype(BF16)
    acc = x1
    for c in range(wup_ref.shape[1] // ff_chunk):
        u = jnp.maximum(_dot(h2, wup_ref[:, c * ff_chunk:(c + 1) * ff_chunk]), 0.0)
        acc = acc + _dot((u * u).astype(BF16), wdn_ref[c * ff_chunk:(c + 1) * ff_chunk, :])
    y_ref[...] = acc


def _merge_mlp(x2d, oa, ob, g, wo, g2, wup, wdn, tm):
    n, d = x2d.shape

    def row(w):
        return pl.BlockSpec((tm, w), lambda i: (i, 0))

    return pl.pallas_call(
        functools.partial(_merge_mlp_kernel, ff_chunk=1024),
        grid=(n // tm,),
        in_specs=[row(d), row(d), row(d), row(2 * d), _const_spec(wo.shape), _const_spec(g2.shape),
                  _const_spec(wup.shape), _const_spec(wdn.shape)],
        out_specs=row(d),
        out_shape=jax.ShapeDtypeStruct((n, d), F32),
        compiler_params=pltpu.CompilerParams(dimension_semantics=("parallel",), vmem_limit_bytes=VMEM_LIMIT),
        name="merge_mlp",
    )(x2d, oa, ob, g, wo, g2, wup, wdn)


def _page_copies_t(pt_ref, cache_ref, buf_ref, sem_ref, seq, slot, n_pages, page):
    return [pltpu.make_async_copy(cache_ref.at[pt_ref[seq * n_pages + p]],
                                  buf_ref.at[slot, :, pl.ds(p * page, page)],
                                  sem_ref.at[slot]) for p in range(n_pages)]


def _samp_score_kernel(pt_ref, qi_ref, wc_ref, kin_ref, cache_ref, o_ref, buf_ref, sem_ref, *, n_pages, page, chunk):
    s = pl.program_id(0)
    ns = pl.num_programs(0)
    slot = s % 2
    past = n_pages * page

    @pl.when(s == 0)
    def _():
        for cp in _page_copies_t(pt_ref, cache_ref, buf_ref, sem_ref, 0, 0, n_pages, page):
            cp.start()

    @pl.when(s + 1 < ns)
    def _():
        for cp in _page_copies_t(pt_ref, cache_ref, buf_ref, sem_ref, s + 1, 1 - slot, n_pages, page):
            cp.start()

    for cp in _page_copies_t(pt_ref, cache_ref, buf_ref, sem_ref, s, slot, n_pages, page):
        cp.wait()

    qi = qi_ref[0]
    wc = wc_ref[0]

    def score_of(keys_t):
        l = _dot(qi, keys_t.astype(BF16))
        return jnp.sum(jnp.maximum(l, 0.0) * wc, axis=0, keepdims=True)

    for c in range(past // chunk):
        o_ref[0, :, c * chunk:(c + 1) * chunk] = score_of(buf_ref[slot, :, c * chunk:(c + 1) * chunk])
    lane = lax.broadcasted_iota(I32, (1, LANES), 1)
    tail_keys = jnp.where(lane == 0, kin_ref[0], 0.0)
    o_ref[0, :, past:] = jnp.where(lane == 0, score_of(tail_keys), NEG_INF)


def _samp_scores(pt_flat, qi3, wc3, kin3, cache_idx_t, n_pages, page):
    ns = qi3.shape[0]
    past = n_pages * page
    chunk = min(2048, past)
    grid_spec = pltpu.PrefetchScalarGridSpec(
        num_scalar_prefetch=1,
        grid=(ns,),
        in_specs=[pl.BlockSpec((1, N_IDX_HEADS, IDX_DIM), lambda s, pt: (s, 0, 0)),
                  pl.BlockSpec((1, N_IDX_HEADS, 1), lambda s, pt: (s, 0, 0)),
                  pl.BlockSpec((1, IDX_DIM, 1), lambda s, pt: (s, 0, 0)),
                  pl.BlockSpec(memory_space=pl.ANY)],
        out_specs=pl.BlockSpec((1, 1, past + LANES), lambda s, pt: (s, 0, 0)),
        scratch_shapes=[pltpu.VMEM((2, IDX_DIM, past), F32), pltpu.SemaphoreType.DMA((2,))],
    )
    return pl.pallas_call(
        functools.partial(_samp_score_kernel, n_pages=n_pages, page=page, chunk=chunk),
        grid_spec=grid_spec,
        out_shape=jax.ShapeDtypeStruct((ns, 1, past + LANES), F32),
        compiler_params=pltpu.CompilerParams(dimension_semantics=("arbitrary",), vmem_limit_bytes=VMEM_LIMIT),
        name="sample_scores",
    )(pt_flat, qi3, wc3, kin3, cache_idx_t)


def _samp_select_kernel(score_ref, ut_ref, j8_ref, idx_ref, np_ref, rank_ref, *, topk, past, chunk):
    ns = score_ref.shape[0]
    sel = _topk_mask(score_ref[...], topk, ut_ref[...])
    off = jnp.zeros((ns, 1), F32)
    for c in range(past // LANES):
        sl = slice(c * LANES, (c + 1) * LANES)
        sc = sel[:, sl]
        pc = _dot(jnp.where(sc, 1.0, 0.0).astype(BF16), ut_ref[...])
        rank_ref[:, sl] = jnp.where(sc, pc + off, 0.0)
        off = off + pc[:, LANES - 1:LANES]
    np_ref[...] = jnp.broadcast_to(off, np_ref.shape).astype(I32)
    want = (lax.broadcasted_iota(I32, (topk, 1), 0) + 1).astype(F32)

    def per_seq(s, carry):
        acc = jnp.zeros((8, topk), F32)
        for c in range(past // chunk):
            rr = rank_ref[pl.ds(s, 1), c * chunk:(c + 1) * chunk]
            onehot = jnp.where(rr == want, 1.0, 0.0).astype(BF16)
            acc = acc + _dot_nt(j8_ref[:, c * chunk:(c + 1) * chunk], onehot)
        idx_ref[pl.ds(s, 1), :] = (acc[0:1, :] * float(LANES) + acc[1:2, :]).astype(I32)
        return carry

    lax.fori_loop(0, ns, per_seq, 0)


def _samp_select(score2d, ut, j8, topk, past):
    ns, n = score2d.shape
    chunk = min(1024, past)
    return pl.pallas_call(
        functools.partial(_samp_select_kernel, topk=topk, past=past, chunk=chunk),
        grid=(1,),
        in_specs=[pl.BlockSpec((ns, n), lambda i: (0, 0)), pl.BlockSpec(ut.shape, lambda i: (0, 0)),
                  pl.BlockSpec(j8.shape, lambda i: (0, 0))],
        out_specs=[pl.BlockSpec((ns, topk), lambda i: (0, 0)), pl.BlockSpec((ns, LANES), lambda i: (0, 0))],
        out_shape=[jax.ShapeDtypeStruct((ns, topk), I32), jax.ShapeDtypeStruct((ns, LANES), I32)],
        scratch_shapes=[pltpu.VMEM((ns, past), F32)],
        compiler_params=pltpu.CompilerParams(dimension_semantics=("arbitrary",), vmem_limit_bytes=VMEM_LIMIT),
        name="sample_select",
    )(score2d, ut, j8)


def _samp_dsa_kernel(pt_ref, idx_ref, np_ref, q_ref, kn_ref, vn_ref, ck_ref, cv_ref, o_ref, kbuf, vbuf, sem_ref,
                     *, n_pages, page, topk):
    s = pl.program_id(0)
    ns = pl.num_programs(0)
    slot = s % 2

    def start_all(seq, sl):
        def body(r, c):
            pos = idx_ref[seq * topk + r]
            pg = pt_ref[seq * n_pages + pos // page]
            off = pos % page
            pltpu.make_async_copy(ck_ref.at[pg, off], kbuf.at[sl, r], sem_ref.at[0, sl]).start()
            pltpu.make_async_copy(cv_ref.at[pg, off], vbuf.at[sl, r], sem_ref.at[1, sl]).start()
            return c
        lax.fori_loop(0, topk, body, 0, unroll=8)

    @pl.when(s == 0)
    def _():
        start_all(0, 0)

    @pl.when(s + 1 < ns)
    def _():
        start_all(s + 1, 1 - slot)

    pltpu.make_async_copy(kbuf.at[1 - slot], kbuf.at[slot], sem_ref.at[0, slot]).wait()
    pltpu.make_async_copy(vbuf.at[1 - slot], vbuf.at[slot], sem_ref.at[1, slot]).wait()

    n_past = np_ref[s]
    slot_ok = lax.broadcasted_iota(I32, (1, topk), 1) < n_past
    new_ok = n_past < topk
    rep = N_HEADS_A // N_KV_A
    for kv in range(N_KV_A):
        kh = kbuf[slot, :, kv, :].astype(BF16)
        vh = vbuf[slot, :, kv, :].astype(BF16)
        q2 = q_ref[0, kv * rep:(kv + 1) * rep, :]
        k_new = kn_ref[0, kv:kv + 1, :].astype(BF16).astype(F32)
        v_new = vn_ref[0, kv:kv + 1, :].astype(BF16).astype(F32)
        sc = jnp.where(slot_ok, _dot_nt(q2, kh), NEG_INF)
        s_new = jnp.sum(q2.astype(F32) * k_new, axis=-1, keepdims=True)
        s_new = jnp.where(new_ok, s_new, NEG_INF)
        m = jnp.maximum(jnp.max(sc, axis=-1, keepdims=True), s_new)
        p = jnp.exp(sc - m)
        p_new = jnp.exp(s_new - m)
        l = jnp.sum(p, axis=-1, keepdims=True) + p_new
        o = (_dot(p.astype(BF16), vh) + p_new.astype(BF16).astype(F32) * v_new) / l
        for g in range(rep):
            hh = kv * rep + g
            o_ref[0, :, hh * LANES:(hh + 1) * LANES] = o[g:g + 1, :]


def _samp_dsa(pt_flat, idx_flat, n_past, q3, kn3, vn3, cache_k, cache_v, n_pages, page, topk):
    ns = q3.shape[0]
    grid_spec = pltpu.PrefetchScalarGridSpec(
        num_scalar_prefetch=3,
        grid=(ns,),
        in_specs=[pl.BlockSpec((1, N_HEADS_A, HEAD_DIM_A), lambda s, *_: (s, 0, 0)),
                  pl.BlockSpec((1, N_KV_A, HEAD_DIM_A), lambda s, *_: (s, 0, 0)),
                  pl.BlockSpec((1, N_KV_A, HEAD_DIM_A), lambda s, *_: (s, 0, 0)),
                  pl.BlockSpec(memory_space=pl.ANY), pl.BlockSpec(memory_space=pl.ANY)],
        out_specs=pl.BlockSpec((1, 1, N_HEADS_A * HEAD_DIM_A), lambda s, *_: (s, 0, 0)),
        scratch_shapes=[pltpu.VMEM((2, topk, N_KV_A, HEAD_DIM_A), F32),
                        pltpu.VMEM((2, topk, N_KV_A, HEAD_DIM_A), F32),
                        pltpu.SemaphoreType.DMA((2, 2))],
    )
    return pl.pallas_call(
        functools.partial(_samp_dsa_kernel, n_pages=n_pages, page=page, topk=topk),
        grid_spec=grid_spec,
        out_shape=jax.ShapeDtypeStruct((ns, 1, N_HEADS_A * HEAD_DIM_A), F32),
        compiler_params=pltpu.CompilerParams(dimension_semantics=("arbitrary",), vmem_limit_bytes=VMEM_LIMIT),
        name="sample_dsa",
    )(pt_flat, idx_flat, n_past, q3, kn3, vn3, cache_k, cache_v)


def _page_copies(pt_ref, cache_ref, buf_ref, sem_ref, seq, slot, n_pages, page):
    return [pltpu.make_async_copy(cache_ref.at[pt_ref[seq * n_pages + p]],
                                  buf_ref.at[slot, pl.ds(p * page, page)],
                                  sem_ref.at[slot]) for p in range(n_pages)]


def _samp_mla_kernel(pt_ref, qn_ref, qcol_ref, gcol_ref, qpe_ref, cn_ref, knn_ref, kpn_ref, wukt_ref, wuktf_ref,
                     wuv_ref, cc_ref, cp_ref, o_ref, cbuf, pbuf, s_ref, sem_ref, *, n_pages, page, chunk):
    s = pl.program_id(0)
    ns = pl.num_programs(0)
    slot = s % 2
    past = n_pages * page
    n_chunks = past // chunk

    def copies(seq, sl):
        return (_page_copies(pt_ref, cc_ref, cbuf, sem_ref.at[0], seq, sl, n_pages, page)
                + _page_copies_t(pt_ref, cp_ref, pbuf, sem_ref.at[1], seq, sl, n_pages, page))

    @pl.when(s == 0)
    def _():
        for cp in copies(0, 0):
            cp.start()

    @pl.when(s + 1 < ns)
    def _():
        for cp in copies(s + 1, 1 - slot):
            cp.start()

    for cp in copies(s, slot):
        cp.wait()

    qg = qcol_ref[0].astype(F32) * gcol_ref[...]
    prod = wuktf_ref[...] * qg
    q_abs = jnp.concatenate([jnp.sum(prod[h * LANES:(h + 1) * LANES, :], axis=0, keepdims=True)
                             for h in range(N_HEADS_M)], axis=0).astype(BF16)
    qpe = qpe_ref[0]

    def score_chunk(c, carry):
        start = pl.multiple_of(c * chunk, chunk)
        cb = cbuf[slot, pl.ds(start, chunk), :].astype(BF16)
        knt = _dot_nt(wukt_ref[...], cb)
        ssq = jnp.concatenate([jnp.sum(jnp.square(knt[h * LANES:(h + 1) * LANES, :]), axis=0, keepdims=True)
                               for h in range(N_HEADS_M)], axis=0)
        s_nope = _dot_nt(q_abs, cb) * lax.rsqrt(ssq * (1.0 / NOPE_DIM) + NORM_EPS)
        s_pe = _dot(qpe, pbuf[slot, :, pl.ds(start, chunk)].astype(BF16))
        s_ref[:, pl.ds(start, chunk)] = s_nope + s_pe
        return carry

    lax.fori_loop(0, n_chunks, score_chunk, 0)
    qn = qn_ref[0].astype(F32)
    s_new = (jnp.sum(qn * knn_ref[0].astype(F32), axis=-1, keepdims=True)
             + jnp.sum(qpe.astype(F32) * kpn_ref[0].astype(BF16).astype(F32), axis=-1, keepdims=True))
    lane = lax.broadcasted_iota(I32, (1, LANES), 1)
    s_ref[:, past:] = jnp.where(lane == 0, s_new, NEG_INF)

    sc = s_ref[...]
    m = jnp.max(sc, axis=-1, keepdims=True)
    p = jnp.exp(sc - m)
    l = jnp.sum(p, axis=-1, keepdims=True)
    s_ref[...] = p
    p_new = jnp.exp(s_new - m)

    def pv_chunk(c, acc):
        start = pl.multiple_of(c * chunk, chunk)
        cb = cbuf[slot, pl.ds(start, chunk), :].astype(BF16)
        return acc + _dot(s_ref[:, pl.ds(start, chunk)].astype(BF16), cb)

    o_lat = lax.fori_loop(0, n_chunks, pv_chunk, jnp.zeros((N_HEADS_M, KV_LORA), F32))
    o_lat = (o_lat + p_new.astype(BF16).astype(F32) * cn_ref[0].astype(BF16).astype(F32)) / l
    o8 = _dot(o_lat.astype(BF16), wuv_ref[...])
    for h in range(N_HEADS_M):
        o_ref[0, :, h * LANES:(h + 1) * LANES] = o8[h:h + 1, h * LANES:(h + 1) * LANES]


def _samp_mla(pt_flat, qn3, qcol, gcol, qpe3, cn3, knn3, kpn3, wukt, wukt_f32, wuv, cache_ckv, cache_kpe_t,
              n_pages, page):
    ns = qn3.shape[0]
    past = n_pages * page
    chunk = min(1024, past)

    def per_seq(shape):
        return pl.BlockSpec((1,) + shape, lambda s, pt: (s, 0, 0))

    def const(shape):
        nd = len(shape)
        return pl.BlockSpec(shape, lambda s, pt: (0,) * nd, pipeline_mode=pl.Buffered(1))

    grid_spec = pltpu.PrefetchScalarGridSpec(
        num_scalar_prefetch=1,
        grid=(ns,),
        in_specs=[per_seq((N_HEADS_M, NOPE_DIM)), per_seq((N_HEADS_M * NOPE_DIM, 1)), const(gcol.shape),
                  per_seq((N_HEADS_M, ROPE_DIM_M)), per_seq((1, KV_LORA)), per_seq((N_HEADS_M, NOPE_DIM)),
                  per_seq((1, ROPE_DIM_M)), const(wukt.shape), const(wukt_f32.shape), const(wuv.shape),
                  pl.BlockSpec(memory_space=pl.ANY), pl.BlockSpec(memory_space=pl.ANY)],
        out_specs=per_seq((1, N_HEADS_M * V_DIM_M)),
        scratch_shapes=[pltpu.VMEM((2, past, KV_LORA), F32), pltpu.VMEM((2, ROPE_DIM_M, past), F32),
                        pltpu.VMEM((N_HEADS_M, past + LANES), F32), pltpu.SemaphoreType.DMA((2, 2))],
    )
    return pl.pallas_call(
        functools.partial(_samp_mla_kernel, n_pages=n_pages, page=page, chunk=chunk),
        grid_spec=grid_spec,
        out_shape=jax.ShapeDtypeStruct((ns, 1, N_HEADS_M * V_DIM_M), F32),
        compiler_params=pltpu.CompilerParams(dimension_semantics=("arbitrary",), vmem_limit_bytes=VMEM_LIMIT),
        name="sample_mla",
    )(pt_flat, qn3, qcol, gcol, qpe3, cn3, knn3, kpn3, wukt, wukt_f32, wuv, cache_ckv, cache_kpe_t)


def _prep_weights(norm1_g, w_in, a_qn_g, a_kn_g, q_a_norm_g, w_q_b, kv_a_norm_g, w_uk, w_uv,
                  m_qn_g, m_kn_g, m_qr_g, m_kr_g):
    d = w_in.shape[0]
    sizes = (1024, 512, 512, 512, IDX_DIM, N_IDX_HEADS, Q_LORA, KV_LORA, ROPE_DIM_M, 2 * d)
    offs = [0]
    for sz in sizes:
        offs.append(offs[-1] + sz)
    wq, wk, wv, wqi, wki, wiw, wqa, wkva, wpe, wg = [w_in[:, offs[i]:offs[i + 1]] for i in range(len(sizes))]
    w_perm = jnp.concatenate(
        [wq, wk, wv, wqi, wki, wpe, wiw, jnp.zeros((d, LANES - N_IDX_HEADS), w_in.dtype), wqa, wkva, wg],
        axis=1).astype(BF16)
    wqb = jnp.concatenate([w_q_b[:, :, :NOPE_DIM].reshape(Q_LORA, -1), w_q_b[:, :, NOPE_DIM:].reshape(Q_LORA, -1)],
                          axis=1).astype(BF16)

    def rowv(v):
        return v.astype(F32).reshape(1, -1)

    return {
        "g1": rowv(norm1_g), "w_in": w_perm, "gq": rowv(a_qn_g), "gk": rowv(a_kn_g), "gqa": rowv(q_a_norm_g),
        "wqb": wqb, "gkva": rowv(kv_a_norm_g), "wuk": w_uk.reshape(KV_LORA, -1).astype(BF16),
        "wuv": w_uv.reshape(KV_LORA, -1).astype(BF16),
        "gqn": rowv(m_qn_g), "gkn": rowv(m_kn_g), "gqr": rowv(jnp.tile(m_qr_g, 2)),
        "gkp": rowv(jnp.concatenate([jnp.ones((IDX_DIM,), F32), m_kr_g.astype(F32)])),
    }


def kernel(x_prompt, x_sample, cache_k, cache_v, cache_idx_k, cache_ckv, cache_kpe, page_table,
           norm1_g, w_in, a_qn_g, a_kn_g, q_a_norm_g, w_q_b, kv_a_norm_g, w_uk, w_uv,
           m_qn_g, m_kn_g, m_qr_g, m_kr_g, w_o, norm2_g, w_up, w_down):
    b, s_len, d = x_prompt.shape
    ns, t_s, _ = x_sample.shape
    page = cache_k.shape[1]
    n_pages = page_table.shape[1]
    past = n_pages * page
    assert d == N_HEADS_A * HEAD_DIM_A and t_s == 1
    assert w_in.shape[1] == C_END - (LANES - N_IDX_HEADS)

    wts = _prep_weights(norm1_g, w_in, a_qn_g, a_kn_g, q_a_norm_g, w_q_b, kv_a_norm_g, w_uk, w_uv,
                        m_qn_g, m_kn_g, m_qr_g, m_kr_g)
    wo = w_o.astype(BF16)
    wup = w_up.astype(BF16)
    wdn = w_down.astype(BF16)
    g2 = norm2_g.astype(F32).reshape(1, -1)
    ut = (jnp.arange(LANES)[:, None] <= jnp.arange(LANES)[None, :]).astype(BF16)

    tm = min(256, s_len)
    qb = min(PROMPT_Q_BLOCK, s_len)
    tab_p = _rope_tables(jnp.arange(s_len))
    xp = x_prompt.reshape(b * s_len, d)
    (q, kf, kb, vf, vb, qi, kif, kpef, kik, kpek, iw, qn, qpe, cf, kn, vm, g) = _project(xp, tab_p, s_len // tm, tm, wts)
    topk_p = min(TOPK_MAX, s_len // 4)
    oa = _dsa_prompt(qi, iw, kik, q, kb, vb, ut, b, s_len, topk_p, qb)
    ob = _mla_prompt(qn, qpe, kn, kpek, vm, b, s_len, qb)
    y_prompt = _merge_mlp(xp, oa, ob, g, wo, g2, wup, wdn, tm).reshape(b, s_len, d)

    tab_s = _rope_tables(jnp.full((ns,), past, I32))
    xs = x_sample.reshape(ns, d)
    (sq, skf, _skb, svf, _svb, sqi, skif, skpef, _skik, _skpek, siw, sqn, sqpe, scf, skn, _svm, sg) = _project(
        xs, tab_s, 1, ns, wts)
    topk_s = min(TOPK_MAX, (past + 1) // 4)
    pt_flat = page_table.reshape(-1).astype(I32)
    cache_idx_t = jnp.swapaxes(cache_idx_k, 1, 2)
    cache_kpe_t = jnp.swapaxes(cache_kpe, 1, 2)
    score = _samp_scores(pt_flat, sqi.reshape(ns, N_IDX_HEADS, IDX_DIM), siw[:, :N_IDX_HEADS].reshape(ns, N_IDX_HEADS, 1),
                         skif.reshape(ns, IDX_DIM, 1), cache_idx_t, n_pages, page)
    pos = jnp.arange(past)
    j8 = jnp.zeros((8, past), F32).at[0].set(pos // LANES).at[1].set(pos % LANES).astype(BF16)
    idx, n_past = _samp_select(score.reshape(ns, past + LANES), ut, j8, topk_s, past)
    oa_s = _samp_dsa(pt_flat, idx.reshape(-1), n_past[:, 0], sq.reshape(ns, N_HEADS_A, HEAD_DIM_A),
                     skf, svf, cache_k, cache_v, n_pages, page, topk_s)
    gcol = jnp.tile(m_kn_g.astype(F32), N_HEADS_M).reshape(-1, 1)
    wukt_f32 = w_uk.reshape(KV_LORA, -1).T.astype(F32)
    ob_s = _samp_mla(pt_flat, sqn.reshape(ns, N_HEADS_M, NOPE_DIM), sqn.reshape(ns, N_HEADS_M * NOPE_DIM, 1), gcol,
                     sqpe.reshape(ns, N_HEADS_M, ROPE_DIM_M), scf.reshape(ns, 1, KV_LORA),
                     skn.reshape(ns, N_HEADS_M, NOPE_DIM), skpef.reshape(ns, 1, ROPE_DIM_M),
                     wukt_f32.astype(BF16), wukt_f32, wts["wuv"], cache_ckv, cache_kpe_t, n_pages, page)
    y_sample = _merge_mlp(xs, oa_s.reshape(ns, d), ob_s.reshape(ns, d), sg, wo, g2, wup, wdn, ns).reshape(ns, 1, d)

    return (y_prompt, y_sample,
            kf.reshape(b, s_len, N_KV_A, HEAD_DIM_A), vf.reshape(b, s_len, N_KV_A, HEAD_DIM_A),
            kif.reshape(b, s_len, IDX_DIM), cf.reshape(b, s_len, KV_LORA), kpef.reshape(b, s_len, ROPE_DIM_M),
            skf.reshape(ns, 1, N_KV_A, HEAD_DIM_A), svf.reshape(ns, 1, N_KV_A, HEAD_DIM_A),
            skif.reshape(ns, 1, IDX_DIM), scf.reshape(ns, 1, KV_LORA), skpef.reshape(ns, 1, ROPE_DIM_M))
```

```python
import functools

import jax
import jax.numpy as jnp
from jax import lax
from jax.experimental import pallas as pl
from jax.experimental.pallas import tpu as pltpu

F32 = jnp.float32
BF16 = jnp.bfloat16
I32 = jnp.int32

N_HEADS_A = 8
HEAD_DIM_A = 128
N_KV_A = 4
ROT_A = 32
N_IDX_HEADS = 8
IDX_DIM = 64
ROT_IDX = 16
TOPK_MAX = 256
N_HEADS_M = 8
Q_LORA = 256
KV_LORA = 256
NOPE_DIM = 128
ROPE_DIM_M = 64
V_DIM_M = 128
DSA_SCALE = HEAD_DIM_A ** -0.5
MLA_SCALE = (NOPE_DIM + ROPE_DIM_M) ** -0.5
ROPE_THETA = 500000.0
NORM_EPS = 1e-6
LANES = 128
NEG_INF = float("-inf")

C_Q, C_K, C_V, C_QI, C_KP, C_IW, C_QA, C_KVA, C_G, C_END = 0, 1024, 1536, 2048, 2560, 2688, 2816, 3072, 3328, 5376

T_A, T_I, T_M, T_KP, N_TAB = 0, 3, 6, 9, 14

VMEM_LIMIT = 56 * 1024 * 1024
PROMPT_Q_BLOCK = 256


def _dot(a, b):
    return jnp.dot(a, b, preferred_element_type=F32)


def _dot_nt(a, b):
    return lax.dot_general(a, b, (((1,), (1,)), ((), ())), preferred_element_type=F32)


def _rms(z, g):
    ms = jnp.mean(z * z, axis=-1, keepdims=True)
    return z * lax.rsqrt(ms + NORM_EPS) * g


def _rope3(y, c, s1, s2, shift):
    return y * c + pltpu.roll(y, shift, 1) * s1 + pltpu.roll(y, LANES - shift, 1) * s2


def _const_spec(shape):
    nd = len(shape)
    return pl.BlockSpec(shape, lambda *_: (0,) * nd, pipeline_mode=pl.Buffered(1))


def _rope_tables(pos):
    pos = pos.astype(F32)
    lane = jnp.arange(LANES)

    def tables(rot, period, lanes_sel=None):
        half = rot // 2
        inv = jnp.power(jnp.float32(ROPE_THETA), -jnp.arange(half, dtype=F32) * (2.0 / rot))
        ang = pos[:, None] * inv[None, :]
        cos, sin = jnp.cos(ang), jnp.sin(ang)
        lp = lane % period
        in_rot = lp < rot
        first = lp < half
        second = in_rot & ~first
        cidx = lp % half
        c = jnp.where(in_rot[None, :], cos[:, cidx], 1.0)
        s1 = jnp.where(second[None, :], sin[:, cidx], 0.0)
        s2 = jnp.where(first[None, :], -sin[:, cidx], 0.0)
        if lanes_sel is not None:
            c = jnp.where(lanes_sel[None, :], c, 1.0)
            s1 = jnp.where(lanes_sel[None, :], s1, 0.0)
            s2 = jnp.where(lanes_sel[None, :], s2, 0.0)
        return c, s1, s2

    ca, s1a, s2a = tables(ROT_A, HEAD_DIM_A)
    ci, s1i, s2i = tables(ROT_IDX, IDX_DIM)
    cm, s1m, s2m = tables(ROPE_DIM_M, ROPE_DIM_M)
    lo = lane < IDX_DIM
    cil, s1il, s2il = tables(ROT_IDX, IDX_DIM, lo)
    cmh, s1mh, s2mh = tables(ROPE_DIM_M, ROPE_DIM_M, ~lo)
    ckp = jnp.where(lo[None, :], cil, cmh)
    return jnp.concatenate([ca, s1a, s2a, ci, s1i, s2i, cm, s1m, s2m, ckp, s1il, s2il, s1mh, s2mh], axis=1)


def _proj_kernel(x_ref, tab_ref, g1_ref, w_ref, gq_ref, gk_ref, gqa_ref, wqb_ref, gkva_ref, wuk_ref, wuv_ref,
                 gqn_ref, gkn_ref, gqr_ref, gkp_ref,
                 q_o, kf_o, kb_o, vf_o, vb_o, qi_o, kif_o, kpef_o, kik_o, kpek_o, iw_o,
                 qn_o, qpe_o, cf_o, kn_o, vm_o, g_o):
    def tab(i):
        return tab_ref[:, i * LANES:(i + 1) * LANES]

    x = x_ref[...]
    ms = jnp.mean(x * x, axis=-1, keepdims=True)
    h = (x * lax.rsqrt(ms + NORM_EPS) * g1_ref[...]).astype(BF16)

    lane = lax.broadcasted_iota(I32, (1, LANES), 1)
    lo = lane < IDX_DIM

    def seg_rms64(z, g):
        sq = z * z
        s_lo = jnp.sum(jnp.where(lo, sq, 0.0), axis=-1, keepdims=True)
        s_hi = jnp.sum(jnp.where(lo, 0.0, sq), axis=-1, keepdims=True)
        ms64 = jnp.where(lo, s_lo, s_hi) * (1.0 / ROPE_DIM_M)
        return z * lax.rsqrt(ms64 + NORM_EPS) * g

    z = _dot(h, w_ref[:, C_Q:C_K])
    for c in range(N_HEADS_A):
        y = _rms(z[:, c * LANES:(c + 1) * LANES], gq_ref[...])
        y = _rope3(y, tab(T_A), tab(T_A + 1), tab(T_A + 2), ROT_A // 2)
        q_o[:, c * LANES:(c + 1) * LANES] = (y * DSA_SCALE).astype(BF16)
    z = _dot(h, w_ref[:, C_K:C_V])
    for c in range(N_KV_A):
        y = _rms(z[:, c * LANES:(c + 1) * LANES], gk_ref[...])
        y = _rope3(y, tab(T_A), tab(T_A + 1), tab(T_A + 2), ROT_A // 2)
        kf_o[:, c, :] = y
        kb_o[:, c * LANES:(c + 1) * LANES] = y.astype(BF16)
    z = _dot(h, w_ref[:, C_V:C_QI])
    for c in range(N_KV_A):
        vf_o[:, c, :] = z[:, c * LANES:(c + 1) * LANES]
    vb_o[...] = z.astype(BF16)
    z = _dot(h, w_ref[:, C_QI:C_KP])
    for c in range(N_IDX_HEADS * IDX_DIM // LANES):
        y = _rope3(z[:, c * LANES:(c + 1) * LANES], tab(T_I), tab(T_I + 1), tab(T_I + 2), ROT_IDX // 2)
        qi_o[:, c * LANES:(c + 1) * LANES] = y.astype(BF16)
    z = _dot(h, w_ref[:, C_KP:C_IW])
    sq = z * z
    s_hi = jnp.sum(jnp.where(lo, 0.0, sq), axis=-1, keepdims=True)
    r_hi = lax.rsqrt(s_hi * (1.0 / ROPE_DIM_M) + NORM_EPS)
    y = jnp.where(lo, z, z * r_hi * gkp_ref[...])
    y = (y * tab(T_KP)
         + pltpu.roll(y, ROT_IDX // 2, 1) * tab(T_KP + 1)
         + pltpu.roll(y, LANES - ROT_IDX // 2, 1) * tab(T_KP + 2)
         + pltpu.roll(y, ROPE_DIM_M // 2, 1) * tab(T_KP + 3)
         + pltpu.roll(y, LANES - ROPE_DIM_M // 2, 1) * tab(T_KP + 4))
    kif_o[...] = y[:, :IDX_DIM]
    kpef_o[...] = y[:, IDX_DIM:]
    ki_lo = jnp.where(lo, y, 0.0)
    kpe_hi = jnp.where(lo, 0.0, y)
    kik_o[:, :LANES] = ki_lo.astype(BF16)
    kik_o[:, LANES:] = pltpu.roll(ki_lo, IDX_DIM, 1).astype(BF16)
    kpek_o[:, :LANES] = pltpu.roll(kpe_hi, IDX_DIM, 1).astype(BF16)
    kpek_o[:, LANES:] = kpe_hi.astype(BF16)
    z = _dot(h, w_ref[:, C_IW:C_QA])
    iw_o[...] = z * (N_IDX_HEADS ** -0.5) * (IDX_DIM ** -0.5)
    z = _dot(h, w_ref[:, C_QA:C_KVA])
    qa = _rms(z, gqa_ref[...]).astype(BF16)
    qm = _dot(qa, wqb_ref[...])
    for c in range(N_HEADS_M):
        y = _rms(qm[:, c * LANES:(c + 1) * LANES], gqn_ref[...])
        qn_o[:, c * LANES:(c + 1) * LANES] = (y * MLA_SCALE).astype(BF16)
    base = N_HEADS_M * NOPE_DIM
    for c in range(N_HEADS_M * ROPE_DIM_M // LANES):
        y = seg_rms64(qm[:, base + c * LANES:base + (c + 1) * LANES], gqr_ref[...])
        y = _rope3(y, tab(T_M), tab(T_M + 1), tab(T_M + 2), ROPE_DIM_M // 2)
        qpe_o[:, c * LANES:(c + 1) * LANES] = (y * MLA_SCALE).astype(BF16)
    z = _dot(h, w_ref[:, C_KVA:C_G])
    ckv = _rms(z, gkva_ref[...])
    cf_o[...] = ckv
    cb = ckv.astype(BF16)
    knr = _dot(cb, wuk_ref[...])
    for c in range(N_HEADS_M):
        y = _rms(knr[:, c * LANES:(c + 1) * LANES], gkn_ref[...])
        kn_o[:, c * LANES:(c + 1) * LANES] = y.astype(BF16)
    vm_o[...] = _dot(cb, wuv_ref[...]).astype(BF16)
    z = _dot(h, w_ref[:, C_G:C_END])
    g_o[...] = 1.0 / (1.0 + jnp.exp(-z))


def _project(x2d, tab, n_pos_tiles, tm, wts):
    n, d = x2d.shape
    grid = (n // tm,)

    def row(width):
        return pl.BlockSpec((tm, width), lambda i: (i, 0))

    kv3 = pl.BlockSpec((tm, N_KV_A, HEAD_DIM_A), lambda i: (i, 0, 0))
    kv3_shape = jax.ShapeDtypeStruct((n, N_KV_A, HEAD_DIM_A), F32)
    out_defs = [
        (1024, BF16), None, (512, BF16), None, (512, BF16), (512, BF16),
        (IDX_DIM, F32), (ROPE_DIM_M, F32), (256, BF16), (256, BF16), (LANES, F32),
        (1024, BF16), (512, BF16), (KV_LORA, F32), (1024, BF16), (1024, BF16), (2048, F32),
    ]
    consts = [wts["g1"], wts["w_in"], wts["gq"], wts["gk"], wts["gqa"], wts["wqb"], wts["gkva"], wts["wuk"],
              wts["wuv"], wts["gqn"], wts["gkn"], wts["gqr"], wts["gkp"]]
    in_specs = [row(d), pl.BlockSpec((tm, N_TAB * LANES), lambda i: (i % n_pos_tiles, 0))]
    in_specs += [_const_spec(c.shape) for c in consts]
    return pl.pallas_call(
        _proj_kernel,
        grid=grid,
        in_specs=in_specs,
        out_specs=[kv3 if od is None else row(od[0]) for od in out_defs],
        out_shape=[kv3_shape if od is None else jax.ShapeDtypeStruct((n, od[0]), od[1]) for od in out_defs],
        compiler_params=pltpu.CompilerParams(dimension_semantics=("parallel",), vmem_limit_bytes=VMEM_LIMIT),
        name="project",
    )(x2d, tab, *consts)


def _topk_mask(score, k, ut):
    r, n = score.shape
    bits = lax.bitcast_convert_type(score, I32)
    key = bits ^ ((bits >> 31) & jnp.int32(0x7FFFFFFF))
    sign = jnp.int32(-2147483648)

    def body(b, t):
        cand_u = t | lax.shift_left(jnp.int32(1), 31 - b)
        cand_s = cand_u ^ sign
        cnt = jnp.sum((key >= cand_s).astype(F32), axis=-1, keepdims=True)
        return jnp.where(cnt >= k, cand_u, t)

    t = lax.fori_loop(0, 32, body, jnp.zeros((r, 1), I32))
    thr = t ^ sign
    gt = key > thr
    eq = key == thr
    need = k - jnp.sum(gt.astype(F32), axis=-1, keepdims=True)
    off = jnp.zeros((r, 1), F32)
    parts = []
    for c in range(n // LANES):
        sl = slice(c * LANES, (c + 1) * LANES)
        eqc = eq[:, sl]
        pc = _dot(jnp.where(eqc, 1.0, 0.0).astype(BF16), ut)
        parts.append(gt[:, sl] | (eqc & ((pc + off) <= need)))
        off = off + pc[:, LANES - 1:LANES]
    return jnp.concatenate(parts, axis=1)


def _prompt_group_calls(kernel_fn, name, q_inputs, seq_inputs, const_inputs, b, s_len, qb):
    nq = s_len // qb
    outs = []
    for g in range(nq):
        n_keys = (g + 1) * qb
        in_specs = [pl.BlockSpec((None, qb, a.shape[2]), lambda bi, g=g: (bi, g, 0)) for a in q_inputs]
        in_specs += [pl.BlockSpec((None, n_keys, a.shape[2]), lambda bi: (bi, 0, 0)) for a in seq_inputs]
        in_specs += [_const_spec(a.shape) for a in const_inputs]
        args = list(q_inputs) + list(seq_inputs) + list(const_inputs)
        outs.append(pl.pallas_call(
            functools.partial(kernel_fn, g=g, qb=qb, n_keys=n_keys),
            grid=(b,),
            in_specs=in_specs,
            out_specs=pl.BlockSpec((None, qb, 1024), lambda bi: (bi, 0, 0)),
            out_shape=jax.ShapeDtypeStruct((b, qb, 1024), BF16),
            compiler_params=pltpu.CompilerParams(dimension_semantics=("parallel",), vmem_limit_bytes=VMEM_LIMIT),
            name=f"{name}_g{g}",
        )(*args))
    return jnp.concatenate(outs, axis=1)


def _dsa_prompt_kernel(qi_ref, iw_ref, q_ref, kik_ref, kb_ref, vb_ref, ut_ref, o_ref, *, topk, g, qb, n_keys):
    qpos = g * qb + lax.broadcasted_iota(I32, (qb, 1), 0)
    kpos = lax.broadcasted_iota(I32, (1, n_keys), 1)
    valid = kpos <= qpos
    if n_keys <= topk:
        mask = valid
    else:
        iw = iw_ref[...]
        score = jnp.zeros((qb, n_keys), F32)
        for j in range(N_IDX_HEADS // 2):
            qs = qi_ref[:, j * LANES:(j + 1) * LANES]
            l0 = _dot_nt(qs, kik_ref[:, :LANES])
            l1 = _dot_nt(qs, kik_ref[:, LANES:])
            score = (score + jnp.maximum(l0, 0.0) * iw[:, 2 * j:2 * j + 1]
                     + jnp.maximum(l1, 0.0) * iw[:, 2 * j + 1:2 * j + 2])
        score = jnp.where(valid, score, NEG_INF)
        mask = _topk_mask(score, topk, ut_ref[...]) & valid
    rep = N_HEADS_A // N_KV_A
    for kv in range(N_KV_A):
        kh = kb_ref[:, kv * LANES:(kv + 1) * LANES]
        vh = vb_ref[:, kv * LANES:(kv + 1) * LANES]
        for r in range(rep):
            hh = kv * rep + r
            s = jnp.where(mask, _dot_nt(q_ref[:, hh * LANES:(hh + 1) * LANES], kh), NEG_INF)
            m = jnp.max(s, axis=-1, keepdims=True)
            p = jnp.exp(s - m)
            l = jnp.sum(p, axis=-1, keepdims=True)
            o_ref[:, hh * LANES:(hh + 1) * LANES] = (_dot(p.astype(BF16), vh) / l).astype(o_ref.dtype)


def _dsa_prompt(qi, iw, kik, q, kb, vb, ut, b, s_len, topk, qb):
    def v3(a):
        return a.reshape(b, s_len, a.shape[-1])

    return _prompt_group_calls(functools.partial(_dsa_prompt_kernel, topk=topk), "dsa_prompt",
                               [v3(qi), v3(iw), v3(q)], [v3(kik), v3(kb), v3(vb)], [ut], b, s_len, qb)


def _mla_prompt_kernel(qn_ref, qpe_ref, kn_ref, kpek_ref, vm_ref, o_ref, *, g, qb, n_keys):
    qpos = g * qb + lax.broadcasted_iota(I32, (qb, 1), 0)
    kpos = lax.broadcasted_iota(I32, (1, n_keys), 1)
    valid = kpos <= qpos
    for h in range(N_HEADS_M):
        sl = slice(h * LANES, (h + 1) * LANES)
        qcat = jnp.concatenate([qn_ref[:, sl], qpe_ref[:, (h // 2) * LANES:(h // 2 + 1) * LANES]], axis=1)
        kcat = jnp.concatenate([kn_ref[:, sl], kpek_ref[:, (h % 2) * LANES:(h % 2 + 1) * LANES]], axis=1)
        s = jnp.where(valid, _dot_nt(qcat, kcat), NEG_INF)
        m = jnp.max(s, axis=-1, keepdims=True)
        p = jnp.exp(s - m)
        l = jnp.sum(p, axis=-1, keepdims=True)
        o_ref[:, sl] = (_dot(p.astype(BF16), vm_ref[:, sl]) / l).astype(o_ref.dtype)


def _mla_prompt(qn, qpe, kn, kpek, vm, b, s_len, qb):
    def v3(a):
        return a.reshape(b, s_len, a.shape[-1])

    return _prompt_group_calls(_mla_prompt_kernel, "mla_prompt", [v3(qn), v3(qpe)], [v3(kn), v3(kpek), v3(vm)], [],
                               b, s_len, qb)


def _merge_mlp_kernel(x_ref, oa_ref, ob_ref, g_ref, wo_ref, g2_ref, wup_ref, wdn_ref, y_ref, *, ff_chunk):
    d = x_ref.shape[1]
    m = (g_ref[:, :d] * oa_ref[...].astype(F32) + g_ref[:, d:] * ob_ref[...].astype(F32)).astype(BF16)
    x1 = x_ref[...] + _dot(m, wo_ref[...])
    ms = jnp.mean(x1 * x1, axis=-1, keepdims=True)
    h2 = (x1 * lax.rsqrt(ms + NORM_EPS) * g2_ref[...]).astype(BF16)
    acc = x1
    for c in range(wup_ref.shape[1] // ff_chunk):
        u = jnp.maximum(_dot(h2, wup_ref[:, c * ff_chunk:(c + 1) * ff_chunk]), 0.0)
        acc = acc + _dot((u * u).astype(BF16), wdn_ref[c * ff_chunk:(c + 1) * ff_chunk, :])
    y_ref[...] = acc


def _merge_mlp(x2d, oa, ob, g, wo, g2, wup, wdn, tm):
    n, d = x2d.shape

    def row(w):
        return pl.BlockSpec((tm, w), lambda i: (i, 0))

    return pl.pallas_call(
        functools.partial(_merge_mlp_kernel, ff_chunk=1024),
        grid=(n // tm,),
        in_specs=[row(d), row(d), row(d), row(2 * d), _const_spec(wo.shape), _const_spec(g2.shape),
                  _const_spec(wup.shape), _const_spec(wdn.shape)],
        out_specs=row(d),
        out_shape=jax.ShapeDtypeStruct((n, d), F32),
        compiler_params=pltpu.CompilerParams(dimension_semantics=("parallel",), vmem_limit_bytes=VMEM_LIMIT),
        name="merge_mlp",
    )(x2d, oa, ob, g, wo, g2, wup, wdn)


def _page_copies_t(pt_ref, cache_ref, buf_ref, sem_ref, seq, slot, n_pages, page):
    return [pltpu.make_async_copy(cache_ref.at[pt_ref[seq * n_pages + p]],
                                  buf_ref.at[slot, :, pl.ds(p * page, page)],
                                  sem_ref.at[slot]) for p in range(n_pages)]


def _samp_score_kernel(pt_ref, qi_ref, wc_ref, kin_ref, cache_ref, o_ref, buf_ref, sem_ref, *, n_pages, page, chunk):
    s = pl.program_id(0)
    ns = pl.num_programs(0)
    slot = s % 2
    past = n_pages * page

    @pl.when(s == 0)
    def _():
        for cp in _page_copies_t(pt_ref, cache_ref, buf_ref, sem_ref, 0, 0, n_pages, page):
            cp.start()

    @pl.when(s + 1 < ns)
    def _():
        for cp in _page_copies_t(pt_ref, cache_ref, buf_ref, sem_ref, s + 1, 1 - slot, n_pages, page):
            cp.start()

    pltpu.make_async_copy(buf_ref.at[1 - slot], buf_ref.at[slot], sem_ref.at[slot]).wait()

    qi = qi_ref[0]
    wc = wc_ref[0]

    def score_of(keys_t):
        l = _dot(qi, keys_t.astype(BF16))
        return jnp.sum(jnp.maximum(l, 0.0) * wc, axis=0, keepdims=True)

    for c in range(past // chunk):
        o_ref[0, :, c * chunk:(c + 1) * chunk] = score_of(buf_ref[slot, :, c * chunk:(c + 1) * chunk])
    lane = lax.broadcasted_iota(I32, (1, LANES), 1)
    tail_keys = jnp.where(lane == 0, kin_ref[0], 0.0)
    o_ref[0, :, past:] = jnp.where(lane == 0, score_of(tail_keys), NEG_INF)


def _samp_scores(pt_flat, qi3, wc3, kin3, cache_idx_t, n_pages, page):
    ns = qi3.shape[0]
    past = n_pages * page
    chunk = min(2048, past)
    grid_spec = pltpu.PrefetchScalarGridSpec(
        num_scalar_prefetch=1,
        grid=(ns,),
        in_specs=[pl.BlockSpec((1, N_IDX_HEADS, IDX_DIM), lambda s, pt: (s, 0, 0)),
                  pl.BlockSpec((1, N_IDX_HEADS, 1), lambda s, pt: (s, 0, 0)),
                  pl.BlockSpec((1, IDX_DIM, 1), lambda s, pt: (s, 0, 0)),
                  pl.BlockSpec(memory_space=pl.ANY)],
        out_specs=pl.BlockSpec((1, 1, past + LANES), lambda s, pt: (s, 0, 0)),
        scratch_shapes=[pltpu.VMEM((2, IDX_DIM, past), F32), pltpu.SemaphoreType.DMA((2,))],
    )
    return pl.pallas_call(
        functools.partial(_samp_score_kernel, n_pages=n_pages, page=page, chunk=chunk),
        grid_spec=grid_spec,
        out_shape=jax.ShapeDtypeStruct((ns, 1, past + LANES), F32),
        compiler_params=pltpu.CompilerParams(dimension_semantics=("arbitrary",), vmem_limit_bytes=VMEM_LIMIT),
        name="sample_scores",
    )(pt_flat, qi3, wc3, kin3, cache_idx_t)


def _samp_select_kernel(score_ref, ut_ref, j8_ref, idx_ref, np_ref, rank_ref, *, topk, past, chunk):
    ns = score_ref.shape[0]
    sel = _topk_mask(score_ref[...], topk, ut_ref[...])
    off = jnp.zeros((ns, 1), F32)
    for c in range(past // LANES):
        sl = slice(c * LANES, (c + 1) * LANES)
        sc = sel[:, sl]
        pc = _dot(jnp.where(sc, 1.0, 0.0).astype(BF16), ut_ref[...])
        rank_ref[:, sl] = jnp.where(sc, pc + off, 0.0)
        off = off + pc[:, LANES - 1:LANES]
    np_ref[...] = jnp.broadcast_to(off, np_ref.shape).astype(I32)
    want = (lax.broadcasted_iota(I32, (topk, 1), 0) + 1).astype(F32)

    def per_seq(s, carry):
        acc = jnp.zeros((8, topk), F32)
        for c in range(past // chunk):
            rr = rank_ref[pl.ds(s, 1), c * chunk:(c + 1) * chunk]
            onehot = jnp.where(rr == want, 1.0, 0.0).astype(BF16)
            acc = acc + _dot_nt(j8_ref[:, c * chunk:(c + 1) * chunk], onehot)
        idx_ref[pl.ds(s, 1), :] = (acc[0:1, :] * float(LANES) + acc[1:2, :]).astype(I32)
        return carry

    lax.fori_loop(0, ns, per_seq, 0)


def _samp_select(score2d, ut, j8, topk, past):
    ns, n = score2d.shape
    chunk = min(1024, past)
    return pl.pallas_call(
        functools.partial(_samp_select_kernel, topk=topk, past=past, chunk=chunk),
        grid=(1,),
        in_specs=[pl.BlockSpec((ns, n), lambda i: (0, 0)), pl.BlockSpec(ut.shape, lambda i: (0, 0)),
                  pl.BlockSpec(j8.shape, lambda i: (0, 0))],
        out_specs=[pl.BlockSpec((ns, topk), lambda i: (0, 0)), pl.BlockSpec((ns, LANES), lambda i: (0, 0))],
        out_shape=[jax.ShapeDtypeStruct((ns, topk), I32), jax.ShapeDtypeStruct((ns, LANES), I32)],
        scratch_shapes=[pltpu.VMEM((ns, past), F32)],
        compiler_params=pltpu.CompilerParams(dimension_semantics=("arbitrary",), vmem_limit_bytes=VMEM_LIMIT),
        name="sample_select",
    )(score2d, ut, j8)


def _samp_dsa_kernel(pt_ref, idx_ref, np_ref, q_ref, kn_ref, vn_ref, ck_ref, cv_ref, o_ref, kbuf, vbuf, sem_ref,
                     *, n_pages, page, topk):
    s = pl.program_id(0)
    ns = pl.num_programs(0)
    slot = s % 2

    def start_all(seq, sl):
        def body(r, c):
            pos = idx_ref[seq * topk + r]
            pg = pt_ref[seq * n_pages + pos // page]
            off = pos % page
            pltpu.make_async_copy(ck_ref.at[pg, off], kbuf.at[sl, r], sem_ref.at[0, sl]).start()
            pltpu.make_async_copy(cv_ref.at[pg, off], vbuf.at[sl, r], sem_ref.at[1, sl]).start()
            return c
        lax.fori_loop(0, topk, body, 0, unroll=8)

    @pl.when(s == 0)
    def _():
        start_all(0, 0)

    @pl.when(s + 1 < ns)
    def _():
        start_all(s + 1, 1 - slot)

    pltpu.make_async_copy(kbuf.at[1 - slot], kbuf.at[slot], sem_ref.at[0, slot]).wait()
    pltpu.make_async_copy(vbuf.at[1 - slot], vbuf.at[slot], sem_ref.at[1, slot]).wait()

    n_past = np_ref[s]
    slot_ok = lax.broadcasted_iota(I32, (1, topk), 1) < n_past
    new_ok = n_past < topk
    rep = N_HEADS_A // N_KV_A
    for kv in range(N_KV_A):
        kh = kbuf[slot, :, kv, :].astype(BF16)
        vh = vbuf[slot, :, kv, :].astype(BF16)
        q2 = q_ref[0, kv * rep:(kv + 1) * rep, :]
        k_new = kn_ref[0, kv:kv + 1, :].astype(BF16).astype(F32)
        v_new = vn_ref[0, kv:kv + 1, :].astype(BF16).astype(F32)
        sc = jnp.where(slot_ok, _dot_nt(q2, kh), NEG_INF)
        s_new = jnp.sum(q2.astype(F32) * k_new, axis=-1, keepdims=True)
        s_new = jnp.where(new_ok, s_new, NEG_INF)
        m = jnp.maximum(jnp.max(sc, axis=-1, keepdims=True), s_new)
        p = jnp.exp(sc - m)
        p_new = jnp.exp(s_new - m)
        l = jnp.sum(p, axis=-1, keepdims=True) + p_new
        o = (_dot(p.astype(BF16), vh) + p_new.astype(BF16).astype(F32) * v_new) / l
        for g in range(rep):
            hh = kv * rep + g
            o_ref[0, :, hh * LANES:(hh + 1) * LANES] = o[g:g + 1, :]


def _samp_dsa(pt_flat, idx_flat, n_past, q3, kn3, vn3, cache_k, cache_v, n_pages, page, topk):
    ns = q3.shape[0]
    grid_spec = pltpu.PrefetchScalarGridSpec(
        num_scalar_prefetch=3,
        grid=(ns,),
        in_specs=[pl.BlockSpec((1, N_HEADS_A, HEAD_DIM_A), lambda s, *_: (s, 0, 0)),
                  pl.BlockSpec((1, N_KV_A, HEAD_DIM_A), lambda s, *_: (s, 0, 0)),
                  pl.BlockSpec((1, N_KV_A, HEAD_DIM_A), lambda s, *_: (s, 0, 0)),
                  pl.BlockSpec(memory_space=pl.ANY), pl.BlockSpec(memory_space=pl.ANY)],
        out_specs=pl.BlockSpec((1, 1, N_HEADS_A * HEAD_DIM_A), lambda s, *_: (s, 0, 0)),
        scratch_shapes=[pltpu.VMEM((2, topk, N_KV_A, HEAD_DIM_A), F32),
                        pltpu.VMEM((2, topk, N_KV_A, HEAD_DIM_A), F32),
                        pltpu.SemaphoreType.DMA((2, 2))],
    )
    return pl.pallas_call(
        functools.partial(_samp_dsa_kernel, n_pages=n_pages, page=page, topk=topk),
        grid_spec=grid_spec,
        out_shape=jax.ShapeDtypeStruct((ns, 1, N_HEADS_A * HEAD_DIM_A), F32),
        compiler_params=pltpu.CompilerParams(dimension_semantics=("arbitrary",), vmem_limit_bytes=VMEM_LIMIT),
        name="sample_dsa",
    )(pt_flat, idx_flat, n_past, q3, kn3, vn3, cache_k, cache_v)


def _page_copies(pt_ref, cache_ref, buf_ref, sem_ref, seq, slot, n_pages, page):
    return [pltpu.make_async_copy(cache_ref.at[pt_ref[seq * n_pages + p]],
                                  buf_ref.at[slot, pl.ds(p * page, page)],
                                  sem_ref.at[slot]) for p in range(n_pages)]


def _samp_mla_kernel(pt_ref, qn_ref, qcol_ref, gcol_ref, qpe_ref, cn_ref, knn_ref, kpn_ref, wukt_ref, wuktf_ref,
                     wuv_ref, cc_ref, cp_ref, o_ref, cbuf, pbuf, s_ref, sem_ref, *, n_pages, page, chunk):
    s = pl.program_id(0)
    ns = pl.num_programs(0)
    slot = s % 2
    past = n_pages * page
    n_chunks = past // chunk

    def copies(seq, sl):
        return (_page_copies(pt_ref, cc_ref, cbuf, sem_ref.at[0], seq, sl, n_pages, page)
                + _page_copies_t(pt_ref, cp_ref, pbuf, sem_ref.at[1], seq, sl, n_pages, page))

    @pl.when(s == 0)
    def _():
        for cp in copies(0, 0):
            cp.start()

    @pl.when(s + 1 < ns)
    def _():
        for cp in copies(s + 1, 1 - slot):
            cp.start()

    pltpu.make_async_copy(cbuf.at[1 - slot], cbuf.at[slot], sem_ref.at[0, slot]).wait()
    pltpu.make_async_copy(pbuf.at[1 - slot], pbuf.at[slot], sem_ref.at[1, slot]).wait()

    qg = qcol_ref[0].astype(F32) * gcol_ref[...]
    prod = wuktf_ref[...] * qg
    q_abs = jnp.concatenate([jnp.sum(prod[h * LANES:(h + 1) * LANES, :], axis=0, keepdims=True)
                             for h in range(N_HEADS_M)], axis=0).astype(BF16)
    qpe = qpe_ref[0]

    def score_chunk(c, carry):
        start = pl.multiple_of(c * chunk, chunk)
        cb = cbuf[slot, pl.ds(start, chunk), :].astype(BF16)
        knt = _dot_nt(wukt_ref[...], cb)
        ssq = jnp.concatenate([jnp.sum(jnp.square(knt[h * LANES:(h + 1) * LANES, :]), axis=0, keepdims=True)
                               for h in range(N_HEADS_M)], axis=0)
        s_nope = _dot_nt(q_abs, cb) * lax.rsqrt(ssq * (1.0 / NOPE_DIM) + NORM_EPS)
        s_pe = _dot(qpe, pbuf[slot, :, pl.ds(start, chunk)].astype(BF16))
        s_ref[:, pl.ds(start, chunk)] = s_nope + s_pe
        return carry

    lax.fori_loop(0, n_chunks, score_chunk, 0)
    qn = qn_ref[0].astype(F32)
    s_new = (jnp.sum(qn * knn_ref[0].astype(F32), axis=-1, keepdims=True)
             + jnp.sum(qpe.astype(F32) * kpn_ref[0].astype(BF16).astype(F32), axis=-1, keepdims=True))
    lane = lax.broadcasted_iota(I32, (1, LANES), 1)
    s_ref[:, past:] = jnp.where(lane == 0, s_new, NEG_INF)

    sc = s_ref[...]
    m = jnp.max(sc, axis=-1, keepdims=True)
    p = jnp.exp(sc - m)
    l = jnp.sum(p, axis=-1, keepdims=True)
    s_ref[...] = p
    p_new = jnp.exp(s_new - m)

    def pv_chunk(c, acc):
        start = pl.multiple_of(c * chunk, chunk)
        cb = cbuf[slot, pl.ds(start, chunk), :].astype(BF16)
        return acc + _dot(s_ref[:, pl.ds(start, chunk)].astype(BF16), cb)

    o_lat = lax.fori_loop(0, n_chunks, pv_chunk, jnp.zeros((N_HEADS_M, KV_LORA), F32))
    o_lat = (o_lat + p_new.astype(BF16).astype(F32) * cn_ref[0].astype(BF16).astype(F32)) / l
    o8 = _dot(o_lat.astype(BF16), wuv_ref[...])
    for h in range(N_HEADS_M):
        o_ref[0, :, h * LANES:(h + 1) * LANES] = o8[h:h + 1, h * LANES:(h + 1) * LANES]


def _samp_mla(pt_flat, qn3, qcol, gcol, qpe3, cn3, knn3, kpn3, wukt, wukt_f32, wuv, cache_ckv, cache_kpe_t,
              n_pages, page):
    ns = qn3.shape[0]
    past = n_pages * page
    chunk = min(1024, past)

    def per_seq(shape):
        return pl.BlockSpec((1,) + shape, lambda s, pt: (s, 0, 0))

    def const(shape):
        nd = len(shape)
        return pl.BlockSpec(shape, lambda s, pt: (0,) * nd, pipeline_mode=pl.Buffered(1))

    grid_spec = pltpu.PrefetchScalarGridSpec(
        num_scalar_prefetch=1,
        grid=(ns,),
        in_specs=[per_seq((N_HEADS_M, NOPE_DIM)), per_seq((N_HEADS_M * NOPE_DIM, 1)), const(gcol.shape),
                  per_seq((N_HEADS_M, ROPE_DIM_M)), per_seq((1, KV_LORA)), per_seq((N_HEADS_M, NOPE_DIM)),
                  per_seq((1, ROPE_DIM_M)), const(wukt.shape), const(wukt_f32.shape), const(wuv.shape),
                  pl.BlockSpec(memory_space=pl.ANY), pl.BlockSpec(memory_space=pl.ANY)],
        out_specs=per_seq((1, N_HEADS_M * V_DIM_M)),
        scratch_shapes=[pltpu.VMEM((2, past, KV_LORA), F32), pltpu.VMEM((2, ROPE_DIM_M, past), F32),
                        pltpu.VMEM((N_HEADS_M, past + LANES), F32), pltpu.SemaphoreType.DMA((2, 2))],
    )
    return pl.pallas_call(
        functools.partial(_samp_mla_kernel, n_pages=n_pages, page=page, chunk=chunk),
        grid_spec=grid_spec,
        out_shape=jax.ShapeDtypeStruct((ns, 1, N_HEADS_M * V_DIM_M), F32),
        compiler_params=pltpu.CompilerParams(dimension_semantics=("arbitrary",), vmem_limit_bytes=VMEM_LIMIT),
        name="sample_mla",
    )(pt_flat, qn3, qcol, gcol, qpe3, cn3, knn3, kpn3, wukt, wukt_f32, wuv, cache_ckv, cache_kpe_t)


def _prep_weights(norm1_g, w_in, a_qn_g, a_kn_g, q_a_norm_g, w_q_b, kv_a_norm_g, w_uk, w_uv,
                  m_qn_g, m_kn_g, m_qr_g, m_kr_g):
    d = w_in.shape[0]
    sizes = (1024, 512, 512, 512, IDX_DIM, N_IDX_HEADS, Q_LORA, KV_LORA, ROPE_DIM_M, 2 * d)
    offs = [0]
    for sz in sizes:
        offs.append(offs[-1] + sz)
    wq, wk, wv, wqi, wki, wiw, wqa, wkva, wpe, wg = [w_in[:, offs[i]:offs[i + 1]] for i in range(len(sizes))]
    w_perm = jnp.concatenate(
        [wq, wk, wv, wqi, wki, wpe, wiw, jnp.zeros((d, LANES - N_IDX_HEADS), w_in.dtype), wqa, wkva, wg],
        axis=1).astype(BF16)
    wqb = jnp.concatenate([w_q_b[:, :, :NOPE_DIM].reshape(Q_LORA, -1), w_q_b[:, :, NOPE_DIM:].reshape(Q_LORA, -1)],
                          axis=1).astype(BF16)

    def rowv(v):
        return v.astype(F32).reshape(1, -1)

    return {
        "g1": rowv(norm1_g), "w_in": w_perm, "gq": rowv(a_qn_g), "gk": rowv(a_kn_g), "gqa": rowv(q_a_norm_g),
        "wqb": wqb, "gkva": rowv(kv_a_norm_g), "wuk": w_uk.reshape(KV_LORA, -1).astype(BF16),
        "wuv": w_uv.reshape(KV_LORA, -1).astype(BF16),
        "gqn": rowv(m_qn_g), "gkn": rowv(m_kn_g), "gqr": rowv(jnp.tile(m_qr_g, 2)),
        "gkp": rowv(jnp.concatenate([jnp.ones((IDX_DIM,), F32), m_kr_g.astype(F32)])),
    }


def kernel(x_prompt, x_sample, cache_k, cache_v, cache_idx_k, cache_ckv, cache_kpe, page_table,
           norm1_g, w_in, a_qn_g, a_kn_g, q_a_norm_g, w_q_b, kv_a_norm_g, w_uk, w_uv,
           m_qn_g, m_kn_g, m_qr_g, m_kr_g, w_o, norm2_g, w_up, w_down):
    b, s_len, d = x_prompt.shape
    ns, t_s, _ = x_sample.shape
    page = cache_k.shape[1]
    n_pages = page_table.shape[1]
    past = n_pages * page
    assert d == N_HEADS_A * HEAD_DIM_A and t_s == 1
    assert w_in.shape[1] == C_END - (LANES - N_IDX_HEADS)

    wts = _prep_weights(norm1_g, w_in, a_qn_g, a_kn_g, q_a_norm_g, w_q_b, kv_a_norm_g, w_uk, w_uv,
                        m_qn_g, m_kn_g, m_qr_g, m_kr_g)
    wo = w_o.astype(BF16)
    wup = w_up.astype(BF16)
    wdn = w_down.astype(BF16)
    g2 = norm2_g.astype(F32).reshape(1, -1)
    ut = (jnp.arange(LANES)[:, None] <= jnp.arange(LANES)[None, :]).astype(BF16)

    tm = min(256, s_len)
    qb = min(PROMPT_Q_BLOCK, s_len)
    tab_p = _rope_tables(jnp.arange(s_len))
    xp = x_prompt.reshape(b * s_len, d)
    (q, kf, kb, vf, vb, qi, kif, kpef, kik, kpek, iw, qn, qpe, cf, kn, vm, g) = _project(xp, tab_p, s_len // tm, tm, wts)
    topk_p = min(TOPK_MAX, s_len // 4)
    oa = _dsa_prompt(qi, iw, kik, q, kb, vb, ut, b, s_len, topk_p, qb).reshape(b * s_len, d)
    ob = _mla_prompt(qn, qpe, kn, kpek, vm, b, s_len, qb).reshape(b * s_len, d)
    y_prompt = _merge_mlp(xp, oa, ob, g, wo, g2, wup, wdn, tm).reshape(b, s_len, d)

    tab_s = _rope_tables(jnp.full((ns,), past, I32))
    xs = x_sample.reshape(ns, d)
    (sq, skf, _skb, svf, _svb, sqi, skif, skpef, _skik, _skpek, siw, sqn, sqpe, scf, skn, _svm, sg) = _project(
        xs, tab_s, 1, ns, wts)
    topk_s = min(TOPK_MAX, (past + 1) // 4)
    pt_flat = page_table.reshape(-1).astype(I32)
    cache_idx_t = jnp.swapaxes(cache_idx_k, 1, 2)
    cache_kpe_t = jnp.swapaxes(cache_kpe, 1, 2)
    score = _samp_scores(pt_flat, sqi.reshape(ns, N_IDX_HEADS, IDX_DIM), siw[:, :N_IDX_HEADS].reshape(ns, N_IDX_HEADS, 1),
                         skif.reshape(ns, IDX_DIM, 1), cache_idx_t, n_pages, page)
    pos = jnp.arange(past)
    j8 = jnp.zeros((8, past), F32).at[0].set(pos // LANES).at[1].set(pos % LANES).astype(BF16)
    idx, n_past = _samp_select(score.reshape(ns, past + LANES), ut, j8, topk_s, past)
    oa_s = _samp_dsa(pt_flat, idx.reshape(-1), n_past[:, 0], sq.reshape(ns, N_HEADS_A, HEAD_DIM_A),
                     skf, svf, cache_k, cache_v, n_pages, page, topk_s)
    gcol = jnp.tile(m_kn_g.astype(F32), N_HEADS_M).reshape(-1, 1)
    wukt_f32 = w_uk.reshape(KV_LORA, -1).T.astype(F32)
    ob_s = _samp_mla(pt_flat, sqn.reshape(ns, N_HEADS_M, NOPE_DIM), sqn.reshape(ns, N_HEADS_M * NOPE_DIM, 1), gcol,
                     sqpe.reshape(ns, N_HEADS_M, ROPE_DIM_M), scf.reshape(ns, 1, KV_LORA),
                     skn.reshape(ns, N_HEADS_M, NOPE_DIM), skpef.reshape(ns, 1, ROPE_DIM_M),
                     wukt_f32.astype(BF16), wukt_f32, wts["wuv"], cache_ckv, cache_kpe_t, n_pages, page)
    y_sample = _merge_mlp(xs, oa_s.reshape(ns, d), ob_s.reshape(ns, d), sg, wo, g2, wup, wdn, ns).reshape(ns, 1, d)

    return (y_prompt, y_sample,
            kf.reshape(b, s_len, N_KV_A, HEAD_DIM_A), vf.reshape(b, s_len, N_KV_A, HEAD_DIM_A),
            kif.reshape(b, s_len, IDX_DIM), cf.reshape(b, s_len, KV_LORA), kpef.reshape(b, s_len, ROPE_DIM_M),
            skf.reshape(ns, 1, N_KV_A, HEAD_DIM_A), svf.reshape(ns, 1, N_KV_A, HEAD_DIM_A),
            skif.reshape(ns, 1, IDX_DIM), scf.reshape(ns, 1, KV_LORA), skpef.reshape(ns, 1, ROPE_DIM_M))
```

```python
import functools

import jax
import jax.numpy as jnp
from jax import lax
from jax.experimental import pallas as pl
from jax.experimental.pallas import tpu as pltpu

F32 = jnp.float32
BF16 = jnp.bfloat16
I32 = jnp.int32

N_HEADS_A = 8
HEAD_DIM_A = 128
N_KV_A = 4
ROT_A = 32
N_IDX_HEADS = 8
IDX_DIM = 64
ROT_IDX = 16
TOPK_MAX = 256
N_HEADS_M = 8
Q_LORA = 256
KV_LORA = 256
NOPE_DIM = 128
ROPE_DIM_M = 64
V_DIM_M = 128
DSA_SCALE = HEAD_DIM_A ** -0.5
MLA_SCALE = (NOPE_DIM + ROPE_DIM_M) ** -0.5
ROPE_THETA = 500000.0
NORM_EPS = 1e-6
LANES = 128
NEG_INF = float("-inf")

C_Q, C_K, C_V, C_QI, C_KP, C_IW, C_QA, C_KVA, C_G, C_END = 0, 1024, 1536, 2048, 2560, 2688, 2816, 3072, 3328, 5376

T_A, T_I, T_M, T_KP, N_TAB = 0, 3, 6, 9, 14

VMEM_LIMIT = 56 * 1024 * 1024
PROMPT_Q_BLOCK = 256


def _dot(a, b):
    return jnp.dot(a, b, preferred_element_type=F32)


def _dot_nt(a, b):
    return lax.dot_general(a, b, (((1,), (1,)), ((), ())), preferred_element_type=F32)


def _rms(z, g):
    ms = jnp.mean(z * z, axis=-1, keepdims=True)
    return z * lax.rsqrt(ms + NORM_EPS) * g


def _rope3(y, c, s1, s2, shift):
    return y * c + pltpu.roll(y, shift, 1) * s1 + pltpu.roll(y, LANES - shift, 1) * s2


def _const_spec(shape):
    nd = len(shape)
    return pl.BlockSpec(shape, lambda *_: (0,) * nd, pipeline_mode=pl.Buffered(1))


def _rope_tables(pos):
    pos = pos.astype(F32)
    lane = jnp.arange(LANES)

    def tables(rot, period, lanes_sel=None):
        half = rot // 2
        inv = jnp.power(jnp.float32(ROPE_THETA), -jnp.arange(half, dtype=F32) * (2.0 / rot))
        ang = pos[:, None] * inv[None, :]
        cos, sin = jnp.cos(ang), jnp.sin(ang)
        lp = lane % period
        in_rot = lp < rot
        first = lp < half
        second = in_rot & ~first
        cidx = lp % half
        c = jnp.where(in_rot[None, :], cos[:, cidx], 1.0)
        s1 = jnp.where(second[None, :], sin[:, cidx], 0.0)
        s2 = jnp.where(first[None, :], -sin[:, cidx], 0.0)
        if lanes_sel is not None:
            c = jnp.where(lanes_sel[None, :], c, 1.0)
            s1 = jnp.where(lanes_sel[None, :], s1, 0.0)
            s2 = jnp.where(lanes_sel[None, :], s2, 0.0)
        return c, s1, s2

    ca, s1a, s2a = tables(ROT_A, HEAD_DIM_A)
    ci, s1i, s2i = tables(ROT_IDX, IDX_DIM)
    cm, s1m, s2m = tables(ROPE_DIM_M, ROPE_DIM_M)
    lo = lane < IDX_DIM
    cil, s1il, s2il = tables(ROT_IDX, IDX_DIM, lo)
    cmh, s1mh, s2mh = tables(ROPE_DIM_M, ROPE_DIM_M, ~lo)
    ckp = jnp.where(lo[None, :], cil, cmh)
    return jnp.concatenate([ca, s1a, s2a, ci, s1i, s2i, cm, s1m, s2m, ckp, s1il, s2il, s1mh, s2mh], axis=1)


def _proj_kernel(x_ref, tab_ref, g1_ref, w_ref, gq_ref, gk_ref, gqa_ref, wqb_ref, gkva_ref, wuk_ref, wuv_ref,
                 gqn_ref, gkn_ref, gqr_ref, gkp_ref,
                 q_o, kf_o, kb_o, vf_o, vb_o, qi_o, kif_o, kpef_o, kik_o, kpek_o, iw_o,
                 qn_o, qpe_o, cf_o, kn_o, vm_o, g_o):
    def tab(i):
        return tab_ref[:, i * LANES:(i + 1) * LANES]

    x = x_ref[...]
    ms = jnp.mean(x * x, axis=-1, keepdims=True)
    h = (x * lax.rsqrt(ms + NORM_EPS) * g1_ref[...]).astype(BF16)

    lane = lax.broadcasted_iota(I32, (1, LANES), 1)
    lo = lane < IDX_DIM

    def seg_rms64(z, g):
        sq = z * z
        s_lo = jnp.sum(jnp.where(lo, sq, 0.0), axis=-1, keepdims=True)
        s_hi = jnp.sum(jnp.where(lo, 0.0, sq), axis=-1, keepdims=True)
        ms64 = jnp.where(lo, s_lo, s_hi) * (1.0 / ROPE_DIM_M)
        return z * lax.rsqrt(ms64 + NORM_EPS) * g

    z = _dot(h, w_ref[:, C_Q:C_K])
    for c in range(N_HEADS_A):
        y = _rms(z[:, c * LANES:(c + 1) * LANES], gq_ref[...])
        y = _rope3(y, tab(T_A), tab(T_A + 1), tab(T_A + 2), ROT_A // 2)
        q_o[:, c * LANES:(c + 1) * LANES] = (y * DSA_SCALE).astype(BF16)
    z = _dot(h, w_ref[:, C_K:C_V])
    for c in range(N_KV_A):
        y = _rms(z[:, c * LANES:(c + 1) * LANES], gk_ref[...])
        y = _rope3(y, tab(T_A), tab(T_A + 1), tab(T_A + 2), ROT_A // 2)
        kf_o[:, c, :] = y
        kb_o[:, c * LANES:(c + 1) * LANES] = y.astype(BF16)
    z = _dot(h, w_ref[:, C_V:C_QI])
    for c in range(N_KV_A):
        vf_o[:, c, :] = z[:, c * LANES:(c + 1) * LANES]
    vb_o[...] = z.astype(BF16)
    z = _dot(h, w_ref[:, C_QI:C_KP])
    for c in range(N_IDX_HEADS * IDX_DIM // LANES):
        y = _rope3(z[:, c * LANES:(c + 1) * LANES], tab(T_I), tab(T_I + 1), tab(T_I + 2), ROT_IDX // 2)
        qi_o[:, c * LANES:(c + 1) * LANES] = y.astype(BF16)
    z = _dot(h, w_ref[:, C_KP:C_IW])
    sq = z * z
    s_hi = jnp.sum(jnp.where(lo, 0.0, sq), axis=-1, keepdims=True)
    r_hi = lax.rsqrt(s_hi * (1.0 / ROPE_DIM_M) + NORM_EPS)
    y = jnp.where(lo, z, z * r_hi * gkp_ref[...])
    y = (y * tab(T_KP)
         + pltpu.roll(y, ROT_IDX // 2, 1) * tab(T_KP + 1)
         + pltpu.roll(y, LANES - ROT_IDX // 2, 1) * tab(T_KP + 2)
         + pltpu.roll(y, ROPE_DIM_M // 2, 1) * tab(T_KP + 3)
         + pltpu.roll(y, LANES - ROPE_DIM_M // 2, 1) * tab(T_KP + 4))
    kif_o[...] = y[:, :IDX_DIM]
    kpef_o[...] = y[:, IDX_DIM:]
    ki_lo = jnp.where(lo, y, 0.0)
    kpe_hi = jnp.where(lo, 0.0, y)
    kik_o[:, :LANES] = ki_lo.astype(BF16)
    kik_o[:, LANES:] = pltpu.roll(ki_lo, IDX_DIM, 1).astype(BF16)
    kpek_o[:, :LANES] = pltpu.roll(kpe_hi, IDX_DIM, 1).astype(BF16)
    kpek_o[:, LANES:] = kpe_hi.astype(BF16)
    z = _dot(h, w_ref[:, C_IW:C_QA])
    iw_o[...] = z * (N_IDX_HEADS ** -0.5) * (IDX_DIM ** -0.5)
    z = _dot(h, w_ref[:, C_QA:C_KVA])
    qa = _rms(z, gqa_ref[...]).astype(BF16)
    qm = _dot(qa, wqb_ref[...])
    for c in range(N_HEADS_M):
        y = _rms(qm[:, c * LANES:(c + 1) * LANES], gqn_ref[...])
        qn_o[:, c * LANES:(c + 1) * LANES] = (y * MLA_SCALE).astype(BF16)
    base = N_HEADS_M * NOPE_DIM
    for c in range(N_HEADS_M * ROPE_DIM_M // LANES):
        y = seg_rms64(qm[:, base + c * LANES:base + (c + 1) * LANES], gqr_ref[...])
        y = _rope3(y, tab(T_M), tab(T_M + 1), tab(T_M + 2), ROPE_DIM_M // 2)
        qpe_o[:, c * LANES:(c + 1) * LANES] = (y * MLA_SCALE).astype(BF16)
    z = _dot(h, w_ref[:, C_KVA:C_G])
    ckv = _rms(z, gkva_ref[...])
    cf_o[...] = ckv
    cb = ckv.astype(BF16)
    knr = _dot(cb, wuk_ref[...])
    for c in range(N_HEADS_M):
        y = _rms(knr[:, c * LANES:(c + 1) * LANES], gkn_ref[...])
        kn_o[:, c * LANES:(c + 1) * LANES] = y.astype(BF16)
    vm_o[...] = _dot(cb, wuv_ref[...]).astype(BF16)
    z = _dot(h, w_ref[:, C_G:C_END])
    g_o[...] = (1.0 / (1.0 + jnp.exp(-z))).astype(g_o.dtype)


def _project(x2d, tab, n_pos_tiles, tm, wts):
    n, d = x2d.shape
    grid = (n // tm,)

    def row(width):
        return pl.BlockSpec((tm, width), lambda i: (i, 0))

    kv3 = pl.BlockSpec((tm, N_KV_A, HEAD_DIM_A), lambda i: (i, 0, 0))
    kv3_shape = jax.ShapeDtypeStruct((n, N_KV_A, HEAD_DIM_A), F32)
    out_defs = [
        (1024, BF16), None, (512, BF16), None, (512, BF16), (512, BF16),
        (IDX_DIM, F32), (ROPE_DIM_M, F32), (256, BF16), (256, BF16), (LANES, F32),
        (1024, BF16), (512, BF16), (KV_LORA, F32), (1024, BF16), (1024, BF16), (2048, BF16),
    ]
    consts = [wts["g1"], wts["w_in"], wts["gq"], wts["gk"], wts["gqa"], wts["wqb"], wts["gkva"], wts["wuk"],
              wts["wuv"], wts["gqn"], wts["gkn"], wts["gqr"], wts["gkp"]]
    in_specs = [row(d), pl.BlockSpec((tm, N_TAB * LANES), lambda i: (i % n_pos_tiles, 0))]
    in_specs += [_const_spec(c.shape) for c in consts]
    return pl.pallas_call(
        _proj_kernel,
        grid=grid,
        in_specs=in_specs,
        out_specs=[kv3 if od is None else row(od[0]) for od in out_defs],
        out_shape=[kv3_shape if od is None else jax.ShapeDtypeStruct((n, od[0]), od[1]) for od in out_defs],
        compiler_params=pltpu.CompilerParams(dimension_semantics=("parallel",), vmem_limit_bytes=VMEM_LIMIT),
        name="project",
    )(x2d, tab, *consts)


def _topk_mask(score, k, ut):
    r, n = score.shape
    bits = lax.bitcast_convert_type(score, I32)
    key = bits ^ ((bits >> 31) & jnp.int32(0x7FFFFFFF))
    sign = jnp.int32(-2147483648)

    def body(b, t):
        cand_u = t | lax.shift_left(jnp.int32(1), 31 - b)
        cand_s = cand_u ^ sign
        cnt = jnp.sum((key >= cand_s).astype(F32), axis=-1, keepdims=True)
        return jnp.where(cnt >= k, cand_u, t)

    t = lax.fori_loop(0, 32, body, jnp.zeros((r, 1), I32))
    thr = t ^ sign
    gt = key > thr
    eq = key == thr
    need = k - jnp.sum(gt.astype(F32), axis=-1, keepdims=True)
    off = jnp.zeros((r, 1), F32)
    parts = []
    for c in range(n // LANES):
        sl = slice(c * LANES, (c + 1) * LANES)
        eqc = eq[:, sl]
        pc = _dot(jnp.where(eqc, 1.0, 0.0).astype(BF16), ut)
        parts.append(gt[:, sl] | (eqc & ((pc + off) <= need)))
        off = off + pc[:, LANES - 1:LANES]
    return jnp.concatenate(parts, axis=1)


def _prompt_group_calls(kernel_fn, name, q_inputs, seq_inputs, const_inputs, b, s_len, qb):
    nq = s_len // qb
    outs = []
    for g in range(nq):
        n_keys = (g + 1) * qb
        in_specs = [pl.BlockSpec((None, qb, a.shape[2]), lambda bi, g=g: (bi, g, 0)) for a in q_inputs]
        in_specs += [pl.BlockSpec((None, n_keys, a.shape[2]), lambda bi: (bi, 0, 0)) for a in seq_inputs]
        in_specs += [_const_spec(a.shape) for a in const_inputs]
        args = list(q_inputs) + list(seq_inputs) + list(const_inputs)
        outs.append(pl.pallas_call(
            functools.partial(kernel_fn, g=g, qb=qb, n_keys=n_keys),
            grid=(b,),
            in_specs=in_specs,
            out_specs=pl.BlockSpec((None, qb, 1024), lambda bi: (bi, 0, 0)),
            out_shape=jax.ShapeDtypeStruct((b, qb, 1024), BF16),
            compiler_params=pltpu.CompilerParams(dimension_semantics=("parallel",), vmem_limit_bytes=VMEM_LIMIT),
            name=f"{name}_g{g}",
        )(*args))
    return outs


def _dsa_prompt_kernel(qi_ref, iw_ref, q_ref, kik_ref, kb_ref, vb_ref, ut_ref, o_ref, *, topk, g, qb, n_keys):
    qpos = g * qb + lax.broadcasted_iota(I32, (qb, 1), 0)
    kpos = lax.broadcasted_iota(I32, (1, n_keys), 1)
    valid = kpos <= qpos
    if n_keys <= topk:
        mask = valid
    else:
        iw = iw_ref[...]
        score = jnp.zeros((qb, n_keys), F32)
        for j in range(N_IDX_HEADS // 2):
            qs = qi_ref[:, j * LANES:(j + 1) * LANES]
            l0 = _dot_nt(qs, kik_ref[:, :LANES])
            l1 = _dot_nt(qs, kik_ref[:, LANES:])
            score = (score + jnp.maximum(l0, 0.0) * iw[:, 2 * j:2 * j + 1]
                     + jnp.maximum(l1, 0.0) * iw[:, 2 * j + 1:2 * j + 2])
        score = jnp.where(valid, score, NEG_INF)
        mask = _topk_mask(score, topk, ut_ref[...]) & valid
    rep = N_HEADS_A // N_KV_A
    for kv in range(N_KV_A):
        kh = kb_ref[:, kv * LANES:(kv + 1) * LANES]
        vh = vb_ref[:, kv * LANES:(kv + 1) * LANES]
        for r in range(rep):
            hh = kv * rep + r
            s = jnp.where(mask, _dot_nt(q_ref[:, hh * LANES:(hh + 1) * LANES], kh), NEG_INF)
            m = jnp.max(s, axis=-1, keepdims=True)
            p = jnp.exp(s - m)
            l = jnp.sum(p, axis=-1, keepdims=True)
            o_ref[:, hh * LANES:(hh + 1) * LANES] = (_dot(p.astype(BF16), vh) / l).astype(o_ref.dtype)


def _dsa_prompt(qi, iw, kik, q, kb, vb, ut, b, s_len, topk, qb):
    def v3(a):
        return a.reshape(b, s_len, a.shape[-1])

    return _prompt_group_calls(functools.partial(_dsa_prompt_kernel, topk=topk), "dsa_prompt",
                               [v3(qi), v3(iw), v3(q)], [v3(kik), v3(kb), v3(vb)], [ut], b, s_len, qb)


def _mla_prompt_kernel(qn_ref, qpe_ref, kn_ref, kpek_ref, vm_ref, o_ref, *, g, qb, n_keys):
    qpos = g * qb + lax.broadcasted_iota(I32, (qb, 1), 0)
    kpos = lax.broadcasted_iota(I32, (1, n_keys), 1)
    valid = kpos <= qpos
    for h in range(N_HEADS_M):
        sl = slice(h * LANES, (h + 1) * LANES)
        qcat = jnp.concatenate([qn_ref[:, sl], qpe_ref[:, (h // 2) * LANES:(h // 2 + 1) * LANES]], axis=1)
        kcat = jnp.concatenate([kn_ref[:, sl], kpek_ref[:, (h % 2) * LANES:(h % 2 + 1) * LANES]], axis=1)
        s = jnp.where(valid, _dot_nt(qcat, kcat), NEG_INF)
        m = jnp.max(s, axis=-1, keepdims=True)
        p = jnp.exp(s - m)
        l = jnp.sum(p, axis=-1, keepdims=True)
        o_ref[:, sl] = (_dot(p.astype(BF16), vm_ref[:, sl]) / l).astype(o_ref.dtype)


def _mla_prompt(qn, qpe, kn, kpek, vm, b, s_len, qb):
    def v3(a):
        return a.reshape(b, s_len, a.shape[-1])

    return _prompt_group_calls(_mla_prompt_kernel, "mla_prompt", [v3(qn), v3(qpe)], [v3(kn), v3(kpek), v3(vm)], [],
                               b, s_len, qb)


def _merge_mlp_kernel(*refs, n_grp, ff_chunk):
    x_ref = refs[0]
    oa_refs = refs[1:1 + n_grp]
    ob_refs = refs[1 + n_grp:1 + 2 * n_grp]
    g_ref, wo_ref, g2_ref, wup_ref, wdn_ref, y_ref, m_ref = refs[1 + 2 * n_grp:]
    d = x_ref.shape[1]
    grp = pl.program_id(0) % n_grp

    def merge(j):
        ga = g_ref[:, :d].astype(F32)
        gb = g_ref[:, d:].astype(F32)
        m_ref[...] = (ga * oa_refs[j][...].astype(F32) + gb * ob_refs[j][...].astype(F32)).astype(BF16)

    for j in range(n_grp):
        pl.when(grp == j)(functools.partial(merge, j))
    x1 = x_ref[...] + _dot(m_ref[...], wo_ref[...])
    ms = jnp.mean(x1 * x1, axis=-1, keepdims=True)
    h2 = (x1 * lax.rsqrt(ms + NORM_EPS) * g2_ref[...]).astype(BF16)
    acc = x1
    for c in range(wup_ref.shape[1] // ff_chunk):
        u = jnp.maximum(_dot(h2, wup_ref[:, c * ff_chunk:(c + 1) * ff_chunk]), 0.0)
        acc = acc + _dot((u * u).astype(BF16), wdn_ref[c * ff_chunk:(c + 1) * ff_chunk, :])
    y_ref[...] = acc


def _merge_mlp(x2d, oa_list, ob_list, g, wo, g2, wup, wdn, tm):
    n, d = x2d.shape
    n_grp = len(oa_list)
    assert len(ob_list) == n_grp and all(a.shape[1:] == (tm, d) for a in oa_list + ob_list)

    def row(w):
        return pl.BlockSpec((tm, w), lambda i: (i, 0))

    grp_spec = pl.BlockSpec((None, tm, d), lambda i: (i // n_grp, 0, 0))
    return pl.pallas_call(
        functools.partial(_merge_mlp_kernel, n_grp=n_grp, ff_chunk=1024),
        grid=(n // tm,),
        in_specs=[row(d)] + [grp_spec] * (2 * n_grp) + [row(2 * d), _const_spec(wo.shape), _const_spec(g2.shape),
                                                        _const_spec(wup.shape), _const_spec(wdn.shape)],
        out_specs=row(d),
        out_shape=jax.ShapeDtypeStruct((n, d), F32),
        scratch_shapes=[pltpu.VMEM((tm, d), BF16)],
        compiler_params=pltpu.CompilerParams(dimension_semantics=("arbitrary",), vmem_limit_bytes=VMEM_LIMIT),
        name="merge_mlp",
    )(x2d, *oa_list, *ob_list, g, wo, g2, wup, wdn)


def _page_copies_t(pt_ref, cache_ref, buf_ref, sem_ref, seq, slot, n_pages, page):
    return [pltpu.make_async_copy(cache_ref.at[pt_ref[seq * n_pages + p]],
                                  buf_ref.at[slot, :, pl.ds(p * page, page)],
                                  sem_ref.at[slot]) for p in range(n_pages)]


def _samp_score_kernel(pt_ref, qi_ref, wc_ref, kin_ref, cache_ref, o_ref, buf_ref, sem_ref, *, n_pages, page, chunk):
    s = pl.program_id(0)
    ns = pl.num_programs(0)
    slot = s % 2
    past = n_pages * page

    @pl.when(s == 0)
    def _():
        for cp in _page_copies_t(pt_ref, cache_ref, buf_ref, sem_ref, 0, 0, n_pages, page):
            cp.start()

    @pl.when(s + 1 < ns)
    def _():
        for cp in _page_copies_t(pt_ref, cache_ref, buf_ref, sem_ref, s + 1, 1 - slot, n_pages, page):
            cp.start()

    pltpu.make_async_copy(buf_ref.at[1 - slot], buf_ref.at[slot], sem_ref.at[slot]).wait()

    qi = qi_ref[0]
    wc = wc_ref[0]

    def score_of(keys_t):
        l = _dot(qi, keys_t.astype(BF16))
        return jnp.sum(jnp.maximum(l, 0.0) * wc, axis=0, keepdims=True)

    for c in range(past // chunk):
        o_ref[0, :, c * chunk:(c + 1) * chunk] = score_of(buf_ref[slot, :, c * chunk:(c + 1) * chunk])
    lane = lax.broadcasted_iota(I32, (1, LANES), 1)
    tail_keys = jnp.where(lane == 0, kin_ref[0], 0.0)
    o_ref[0, :, past:] = jnp.where(lane == 0, score_of(tail_keys), NEG_INF)


def _samp_scores(pt_flat, qi3, wc3, kin3, cache_idx_t, n_pages, page):
    ns = qi3.shape[0]
    past = n_pages * page
    chunk = min(2048, past)
    grid_spec = pltpu.PrefetchScalarGridSpec(
        num_scalar_prefetch=1,
        grid=(ns,),
        in_specs=[pl.BlockSpec((1, N_IDX_HEADS, IDX_DIM), lambda s, pt: (s, 0, 0)),
                  pl.BlockSpec((1, N_IDX_HEADS, 1), lambda s, pt: (s, 0, 0)),
                  pl.BlockSpec((1, IDX_DIM, 1), lambda s, pt: (s, 0, 0)),
                  pl.BlockSpec(memory_space=pl.ANY)],
        out_specs=pl.BlockSpec((1, 1, past + LANES), lambda s, pt: (s, 0, 0)),
        scratch_shapes=[pltpu.VMEM((2, IDX_DIM, past), F32), pltpu.SemaphoreType.DMA((2,))],
    )
    return pl.pallas_call(
        functools.partial(_samp_score_kernel, n_pages=n_pages, page=page, chunk=chunk),
        grid_spec=grid_spec,
        out_shape=jax.ShapeDtypeStruct((ns, 1, past + LANES), F32),
        compiler_params=pltpu.CompilerParams(dimension_semantics=("arbitrary",), vmem_limit_bytes=VMEM_LIMIT),
        name="sample_scores",
    )(pt_flat, qi3, wc3, kin3, cache_idx_t)


def _samp_select_kernel(score_ref, ut_ref, j8_ref, idx_ref, np_ref, rank_ref, *, topk, past, chunk):
    ns = score_ref.shape[0]
    sel = _topk_mask(score_ref[...], topk, ut_ref[...])
    off = jnp.zeros((ns, 1), F32)
    for c in range(past // LANES):
        sl = slice(c * LANES, (c + 1) * LANES)
        sc = sel[:, sl]
        pc = _dot(jnp.where(sc, 1.0, 0.0).astype(BF16), ut_ref[...])
        rank_ref[:, sl] = jnp.where(sc, pc + off, 0.0)
        off = off + pc[:, LANES - 1:LANES]
    np_ref[...] = jnp.broadcast_to(off, np_ref.shape).astype(I32)
    want = (lax.broadcasted_iota(I32, (topk, 1), 0) + 1).astype(F32)

    def per_seq(s, carry):
        acc = jnp.zeros((8, topk), F32)
        for c in range(past // chunk):
            rr = rank_ref[pl.ds(s, 1), c * chunk:(c + 1) * chunk]
            onehot = jnp.where(rr == want, 1.0, 0.0).astype(BF16)
            acc = acc + _dot_nt(j8_ref[:, c * chunk:(c + 1) * chunk], onehot)
        idx_ref[pl.ds(s, 1), :] = (acc[0:1, :] * float(LANES) + acc[1:2, :]).astype(I32)
        return carry

    lax.fori_loop(0, ns, per_seq, 0)


def _samp_select(score2d, ut, j8, topk, past):
    ns, n = score2d.shape
    chunk = min(1024, past)
    return pl.pallas_call(
        functools.partial(_samp_select_kernel, topk=topk, past=past, chunk=chunk),
        grid=(1,),
        in_specs=[pl.BlockSpec((ns, n), lambda i: (0, 0)), pl.BlockSpec(ut.shape, lambda i: (0, 0)),
                  pl.BlockSpec(j8.shape, lambda i: (0, 0))],
        out_specs=[pl.BlockSpec((ns, topk), lambda i: (0, 0)), pl.BlockSpec((ns, LANES), lambda i: (0, 0))],
        out_shape=[jax.ShapeDtypeStruct((ns, topk), I32), jax.ShapeDtypeStruct((ns, LANES), I32)],
        scratch_shapes=[pltpu.VMEM((ns, past), F32)],
        compiler_params=pltpu.CompilerParams(dimension_semantics=("arbitrary",), vmem_limit_bytes=VMEM_LIMIT),
        name="sample_select",
    )(score2d, ut, j8)


def _dsa_rows_attend(q_ref, kn_ref, vn_ref, kbuf, vbuf, o_ref, n_past, topk):
    slot_ok = lax.broadcasted_iota(I32, (1, topk), 1) < n_past
    new_ok = n_past < topk
    rep = N_HEADS_A // N_KV_A
    for kv in range(N_KV_A):
        kh = kbuf[:, kv, :].astype(BF16)
        vh = vbuf[:, kv, :].astype(BF16)
        q2 = q_ref[0, kv * rep:(kv + 1) * rep, :]
        k_new = kn_ref[0, kv:kv + 1, :].astype(BF16).astype(F32)
        v_new = vn_ref[0, kv:kv + 1, :].astype(BF16).astype(F32)
        sc = jnp.where(slot_ok, _dot_nt(q2, kh), NEG_INF)
        s_new = jnp.sum(q2.astype(F32) * k_new, axis=-1, keepdims=True)
        s_new = jnp.where(new_ok, s_new, NEG_INF)
        m = jnp.maximum(jnp.max(sc, axis=-1, keepdims=True), s_new)
        p = jnp.exp(sc - m)
        p_new = jnp.exp(s_new - m)
        l = jnp.sum(p, axis=-1, keepdims=True) + p_new
        o = (_dot(p.astype(BF16), vh) + p_new.astype(BF16).astype(F32) * v_new) / l
        for g in range(rep):
            hh = kv * rep + g
            o_ref[0, :, hh * LANES:(hh + 1) * LANES] = o[g:g + 1, :]


def _page_copies(pt_ref, cache_ref, buf_ref, sem_ref, seq, slot, n_pages, page):
    return [pltpu.make_async_copy(cache_ref.at[pt_ref[seq * n_pages + p]],
                                  buf_ref.at[slot, pl.ds(p * page, page)],
                                  sem_ref.at[slot]) for p in range(n_pages)]


def _samp_attn_kernel(pt_ref, idx_ref, np_ref, qn_ref, qcol_ref, gcol_ref, qpe_ref, cn_ref, knn_ref, kpn_ref,
                      wukt_ref, wuktf_ref, wuv_ref, qa_ref, kna_ref, vna_ref, cc_ref, cp_ref, ck_ref, cv_ref,
                      o_ref, oa_ref, cbuf, pbuf, s_ref, kbuf, vbuf, sem_ref, row_sem, *, n_pages, page, chunk, topk):
    s = pl.program_id(0)
    ns = pl.num_programs(0)
    slot = s % 2
    past = n_pages * page
    n_chunks = past // chunk
    rows_per_chunk = topk // n_chunks

    def copies(seq, sl):
        return (_page_copies(pt_ref, cc_ref, cbuf, sem_ref.at[0], seq, sl, n_pages, page)
                + _page_copies_t(pt_ref, cp_ref, pbuf, sem_ref.at[1], seq, sl, n_pages, page))

    @pl.when(s == 0)
    def _():
        for cp in copies(0, 0):
            cp.start()

    @pl.when(s + 1 < ns)
    def _():
        for cp in copies(s + 1, 1 - slot):
            cp.start()

    pltpu.make_async_copy(cbuf.at[1 - slot], cbuf.at[slot], sem_ref.at[0, slot]).wait()
    pltpu.make_async_copy(pbuf.at[1 - slot], pbuf.at[slot], sem_ref.at[1, slot]).wait()

    qg = qcol_ref[0].astype(F32) * gcol_ref[...]
    prod = wuktf_ref[...] * qg
    q_abs = jnp.concatenate([jnp.sum(prod[h * LANES:(h + 1) * LANES, :], axis=0, keepdims=True)
                             for h in range(N_HEADS_M)], axis=0).astype(BF16)
    qpe = qpe_ref[0]

    def score_chunk(c, carry):
        for r in range(rows_per_chunk):
            row = c * rows_per_chunk + r
            pos = idx_ref[s * topk + row]
            pg = pt_ref[s * n_pages + pos // page]
            off = pos % page
            pltpu.make_async_copy(ck_ref.at[pg, off], kbuf.at[row], row_sem.at[0]).start()
            pltpu.make_async_copy(cv_ref.at[pg, off], vbuf.at[row], row_sem.at[1]).start()
        start = pl.multiple_of(c * chunk, chunk)
        cb = cbuf[slot, pl.ds(start, chunk), :].astype(BF16)
        knt = _dot_nt(wukt_ref[...], cb)
        ssq = jnp.concatenate([jnp.sum(jnp.square(knt[h * LANES:(h + 1) * LANES, :]), axis=0, keepdims=True)
                               for h in range(N_HEADS_M)], axis=0)
        s_nope = _dot_nt(q_abs, cb) * lax.rsqrt(ssq * (1.0 / NOPE_DIM) + NORM_EPS)
        s_pe = _dot(qpe, pbuf[slot, :, pl.ds(start, chunk)].astype(BF16))
        s_ref[:, pl.ds(start, chunk)] = s_nope + s_pe
        return carry

    lax.fori_loop(0, n_chunks, score_chunk, 0)
    qn = qn_ref[0].astype(F32)
    s_new = (jnp.sum(qn * knn_ref[0].astype(F32), axis=-1, keepdims=True)
             + jnp.sum(qpe.astype(F32) * kpn_ref[0].astype(BF16).astype(F32), axis=-1, keepdims=True))
    lane = lax.broadcasted_iota(I32, (1, LANES), 1)
    s_ref[:, past:] = jnp.where(lane == 0, s_new, NEG_INF)

    sc = s_ref[...]
    m = jnp.max(sc, axis=-1, keepdims=True)
    p = jnp.exp(sc - m)
    l = jnp.sum(p, axis=-1, keepdims=True)
    s_ref[...] = p
    p_new = jnp.exp(s_new - m)

    def pv_chunk(c, acc):
        start = pl.multiple_of(c * chunk, chunk)
        cb = cbuf[slot, pl.ds(start, chunk), :].astype(BF16)
        return acc + _dot(s_ref[:, pl.ds(start, chunk)].astype(BF16), cb)

    o_lat = lax.fori_loop(0, n_chunks, pv_chunk, jnp.zeros((N_HEADS_M, KV_LORA), F32))
    o_lat = (o_lat + p_new.astype(BF16).astype(F32) * cn_ref[0].astype(BF16).astype(F32)) / l
    o8 = _dot(o_lat.astype(BF16), wuv_ref[...])
    for h in range(N_HEADS_M):
        o_ref[0, :, h * LANES:(h + 1) * LANES] = o8[h:h + 1, h * LANES:(h + 1) * LANES]

    pltpu.make_async_copy(vbuf, kbuf, row_sem.at[0]).wait()
    pltpu.make_async_copy(kbuf, vbuf, row_sem.at[1]).wait()
    _dsa_rows_attend(qa_ref, kna_ref, vna_ref, kbuf, vbuf, oa_ref, np_ref[s], topk)


def _samp_attn(pt_flat, idx_flat, n_past, qn3, qcol, gcol, qpe3, cn3, knn3, kpn3, wukt, wukt_f32, wuv,
               qa3, kna3, vna3, cache_ckv, cache_kpe_t, cache_k, cache_v, n_pages, page, topk):
    ns = qn3.shape[0]
    past = n_pages * page
    chunk = min(1024, past)
    assert topk % (past // chunk) == 0

    def per_seq(shape):
        return pl.BlockSpec((1,) + shape, lambda s, *_: (s, 0, 0))

    def const(shape):
        nd = len(shape)
        return pl.BlockSpec(shape, lambda s, *_: (0,) * nd, pipeline_mode=pl.Buffered(1))

    hbm = pl.BlockSpec(memory_space=pl.ANY)
    grid_spec = pltpu.PrefetchScalarGridSpec(
        num_scalar_prefetch=3,
        grid=(ns,),
        in_specs=[per_seq((N_HEADS_M, NOPE_DIM)), per_seq((N_HEADS_M * NOPE_DIM, 1)), const(gcol.shape),
                  per_seq((N_HEADS_M, ROPE_DIM_M)), per_seq((1, KV_LORA)), per_seq((N_HEADS_M, NOPE_DIM)),
                  per_seq((1, ROPE_DIM_M)), const(wukt.shape), const(wukt_f32.shape), const(wuv.shape),
                  per_seq((N_HEADS_A, HEAD_DIM_A)), per_seq((N_KV_A, HEAD_DIM_A)), per_seq((N_KV_A, HEAD_DIM_A)),
                  hbm, hbm, hbm, hbm],
        out_specs=[per_seq((1, N_HEADS_M * V_DIM_M)), per_seq((1, N_HEADS_A * HEAD_DIM_A))],
        scratch_shapes=[pltpu.VMEM((2, past, KV_LORA), F32), pltpu.VMEM((2, ROPE_DIM_M, past), F32),
                        pltpu.VMEM((N_HEADS_M, past + LANES), F32),
                        pltpu.VMEM((topk, N_KV_A, HEAD_DIM_A), F32), pltpu.VMEM((topk, N_KV_A, HEAD_DIM_A), F32),
                        pltpu.SemaphoreType.DMA((2, 2)), pltpu.SemaphoreType.DMA((2,))],
    )
    out_row = jax.ShapeDtypeStruct((ns, 1, N_HEADS_M * V_DIM_M), F32)
    return pl.pallas_call(
        functools.partial(_samp_attn_kernel, n_pages=n_pages, page=page, chunk=chunk, topk=topk),
        grid_spec=grid_spec,
        out_shape=[out_row, out_row],
        compiler_params=pltpu.CompilerParams(dimension_semantics=("arbitrary",), vmem_limit_bytes=VMEM_LIMIT),
        name="sample_attn",
    )(pt_flat, idx_flat, n_past, qn3, qcol, gcol, qpe3, cn3, knn3, kpn3, wukt, wukt_f32, wuv,
      qa3, kna3, vna3, cache_ckv, cache_kpe_t, cache_k, cache_v)


def _prep_weights(norm1_g, w_in, a_qn_g, a_kn_g, q_a_norm_g, w_q_b, kv_a_norm_g, w_uk, w_uv,
                  m_qn_g, m_kn_g, m_qr_g, m_kr_g):
    d = w_in.shape[0]
    sizes = (1024, 512, 512, 512, IDX_DIM, N_IDX_HEADS, Q_LORA, KV_LORA, ROPE_DIM_M, 2 * d)
    offs = [0]
    for sz in sizes:
        offs.append(offs[-1] + sz)
    wq, wk, wv, wqi, wki, wiw, wqa, wkva, wpe, wg = [w_in[:, offs[i]:offs[i + 1]] for i in range(len(sizes))]
    w_perm = jnp.concatenate(
        [wq, wk, wv, wqi, wki, wpe, wiw, jnp.zeros((d, LANES - N_IDX_HEADS), w_in.dtype), wqa, wkva, wg],
        axis=1).astype(BF16)
    wqb = jnp.concatenate([w_q_b[:, :, :NOPE_DIM].reshape(Q_LORA, -1), w_q_b[:, :, NOPE_DIM:].reshape(Q_LORA, -1)],
                          axis=1).astype(BF16)

    def rowv(v):
        return v.astype(F32).reshape(1, -1)

    return {
        "g1": rowv(norm1_g), "w_in": w_perm, "gq": rowv(a_qn_g), "gk": rowv(a_kn_g), "gqa": rowv(q_a_norm_g),
        "wqb": wqb, "gkva": rowv(kv_a_norm_g), "wuk": w_uk.reshape(KV_LORA, -1).astype(BF16),
        "wuv": w_uv.reshape(KV_LORA, -1).astype(BF16),
        "gqn": rowv(m_qn_g), "gkn": rowv(m_kn_g), "gqr": rowv(jnp.tile(m_qr_g, 2)),
        "gkp": rowv(jnp.concatenate([jnp.ones((IDX_DIM,), F32), m_kr_g.astype(F32)])),
    }


def kernel(x_prompt, x_sample, cache_k, cache_v, cache_idx_k, cache_ckv, cache_kpe, page_table,
           norm1_g, w_in, a_qn_g, a_kn_g, q_a_norm_g, w_q_b, kv_a_norm_g, w_uk, w_uv,
           m_qn_g, m_kn_g, m_qr_g, m_kr_g, w_o, norm2_g, w_up, w_down):
    b, s_len, d = x_prompt.shape
    ns, t_s, _ = x_sample.shape
    page = cache_k.shape[1]
    n_pages = page_table.shape[1]
    past = n_pages * page
    assert d == N_HEADS_A * HEAD_DIM_A and t_s == 1
    assert w_in.shape[1] == C_END - (LANES - N_IDX_HEADS)

    wts = _prep_weights(norm1_g, w_in, a_qn_g, a_kn_g, q_a_norm_g, w_q_b, kv_a_norm_g, w_uk, w_uv,
                        m_qn_g, m_kn_g, m_qr_g, m_kr_g)
    wo = w_o.astype(BF16)
    wup = w_up.astype(BF16)
    wdn = w_down.astype(BF16)
    g2 = norm2_g.astype(F32).reshape(1, -1)
    ut = (jnp.arange(LANES)[:, None] <= jnp.arange(LANES)[None, :]).astype(BF16)

    tm = min(256, s_len)
    qb = min(PROMPT_Q_BLOCK, s_len)
    tab_p = _rope_tables(jnp.arange(s_len))
    xp = x_prompt.reshape(b * s_len, d)
    (q, kf, kb, vf, vb, qi, kif, kpef, kik, kpek, iw, qn, qpe, cf, kn, vm, g) = _project(xp, tab_p, s_len // tm, tm, wts)
    topk_p = min(TOPK_MAX, s_len // 4)
    oa = _dsa_prompt(qi, iw, kik, q, kb, vb, ut, b, s_len, topk_p, qb)
    ob = _mla_prompt(qn, qpe, kn, kpek, vm, b, s_len, qb)
    y_prompt = _merge_mlp(xp, oa, ob, g, wo, g2, wup, wdn, qb).reshape(b, s_len, d)

    tab_s = _rope_tables(jnp.full((ns,), past, I32))
    xs = x_sample.reshape(ns, d)
    (sq, skf, _skb, svf, _svb, sqi, skif, skpef, _skik, _skpek, siw, sqn, sqpe, scf, skn, _svm, sg) = _project(
        xs, tab_s, 1, ns, wts)
    topk_s = min(TOPK_MAX, (past + 1) // 4)
    pt_flat = page_table.reshape(-1).astype(I32)
    cache_idx_t = jnp.swapaxes(cache_idx_k, 1, 2)
    cache_kpe_t = jnp.swapaxes(cache_kpe, 1, 2)
    score = _samp_scores(pt_flat, sqi.reshape(ns, N_IDX_HEADS, IDX_DIM), siw[:, :N_IDX_HEADS].reshape(ns, N_IDX_HEADS, 1),
                         skif.reshape(ns, IDX_DIM, 1), cache_idx_t, n_pages, page)
    pos = jnp.arange(past)
    j8 = jnp.zeros((8, past), F32).at[0].set(pos // LANES).at[1].set(pos % LANES).astype(BF16)
    idx, n_past = _samp_select(score.reshape(ns, past + LANES), ut, j8, topk_s, past)
    gcol = jnp.tile(m_kn_g.astype(F32), N_HEADS_M).reshape(-1, 1)
    wukt_f32 = w_uk.reshape(KV_LORA, -1).T.astype(F32)
    ob_s, oa_s = _samp_attn(pt_flat, idx.reshape(-1), n_past[:, 0],
                            sqn.reshape(ns, N_HEADS_M, NOPE_DIM), sqn.reshape(ns, N_HEADS_M * NOPE_DIM, 1), gcol,
                            sqpe.reshape(ns, N_HEADS_M, ROPE_DIM_M), scf.reshape(ns, 1, KV_LORA),
                            skn.reshape(ns, N_HEADS_M, NOPE_DIM), skpef.reshape(ns, 1, ROPE_DIM_M),
                            wukt_f32.astype(BF16), wukt_f32, wts["wuv"],
                            sq.reshape(ns, N_HEADS_A, HEAD_DIM_A), skf, svf,
                            cache_ckv, cache_kpe_t, cache_k, cache_v, n_pages, page, topk_s)
    y_sample = _merge_mlp(xs, [oa_s.reshape(1, ns, d)], [ob_s.reshape(1, ns, d)], sg, wo, g2, wup, wdn,
                          ns).reshape(ns, 1, d)

    return (y_prompt, y_sample,
            kf.reshape(b, s_len, N_KV_A, HEAD_DIM_A), vf.reshape(b, s_len, N_KV_A, HEAD_DIM_A),
            kif.reshape(b, s_len, IDX_DIM), cf.reshape(b, s_len, KV_LORA), kpef.reshape(b, s_len, ROPE_DIM_M),
            skf.reshape(ns, 1, N_KV_A, HEAD_DIM_A), svf.reshape(ns, 1, N_KV_A, HEAD_DIM_A),
            skif.reshape(ns, 1, IDX_DIM), scf.reshape(ns, 1, KV_LORA), skpef.reshape(ns, 1, ROPE_DIM_M))
```

```python
import functools

import jax
import jax.numpy as jnp
from jax import lax
from jax.experimental import pallas as pl
from jax.experimental.pallas import tpu as pltpu

F32 = jnp.float32
BF16 = jnp.bfloat16
I32 = jnp.int32

N_HEADS_A = 8
HEAD_DIM_A = 128
N_KV_A = 4
ROT_A = 32
N_IDX_HEADS = 8
IDX_DIM = 64
ROT_IDX = 16
TOPK_MAX = 256
N_HEADS_M = 8
Q_LORA = 256
KV_LORA = 256
NOPE_DIM = 128
ROPE_DIM_M = 64
V_DIM_M = 128
DSA_SCALE = HEAD_DIM_A ** -0.5
MLA_SCALE = (NOPE_DIM + ROPE_DIM_M) ** -0.5
ROPE_THETA = 500000.0
NORM_EPS = 1e-6
LANES = 128
NEG_INF = float("-inf")
NEG_FILL = -3.0e38

C_Q, C_K, C_V, C_QI, C_KP, C_IW, C_QA, C_KVA, C_G, C_END = 0, 1024, 1536, 2048, 2560, 2688, 2816, 3072, 3328, 5376

T_A, T_I, T_M, T_KP, N_TAB = 0, 3, 6, 9, 14

VMEM_LIMIT = 56 * 1024 * 1024
PROMPT_Q_BLOCK = 256
SEARCH_UNROLL = 4


def _dot(a, b):
    return jnp.dot(a, b, preferred_element_type=F32)


def _dot_nt(a, b):
    return lax.dot_general(a, b, (((1,), (1,)), ((), ())), preferred_element_type=F32)


def _rms(z, g):
    ms = jnp.mean(z * z, axis=-1, keepdims=True)
    return z * lax.rsqrt(ms + NORM_EPS) * g


def _rope3(y, c, s1, s2, shift):
    return y * c + pltpu.roll(y, shift, 1) * s1 + pltpu.roll(y, LANES - shift, 1) * s2


def _const_spec(shape):
    nd = len(shape)
    return pl.BlockSpec(shape, lambda *_: (0,) * nd, pipeline_mode=pl.Buffered(1))


def _rope_tables(pos):
    pos = pos.astype(F32)
    lane = jnp.arange(LANES)

    def tables(rot, period, lanes_sel=None):
        half = rot // 2
        inv = jnp.power(jnp.float32(ROPE_THETA), -jnp.arange(half, dtype=F32) * (2.0 / rot))
        ang = pos[:, None] * inv[None, :]
        cos, sin = jnp.cos(ang), jnp.sin(ang)
        lp = lane % period
        in_rot = lp < rot
        first = lp < half
        second = in_rot & ~first
        cidx = lp % half
        c = jnp.where(in_rot[None, :], cos[:, cidx], 1.0)
        s1 = jnp.where(second[None, :], sin[:, cidx], 0.0)
        s2 = jnp.where(first[None, :], -sin[:, cidx], 0.0)
        if lanes_sel is not None:
            c = jnp.where(lanes_sel[None, :], c, 1.0)
            s1 = jnp.where(lanes_sel[None, :], s1, 0.0)
            s2 = jnp.where(lanes_sel[None, :], s2, 0.0)
        return c, s1, s2

    ca, s1a, s2a = tables(ROT_A, HEAD_DIM_A)
    ci, s1i, s2i = tables(ROT_IDX, IDX_DIM)
    cm, s1m, s2m = tables(ROPE_DIM_M, ROPE_DIM_M)
    lo = lane < IDX_DIM
    cil, s1il, s2il = tables(ROT_IDX, IDX_DIM, lo)
    cmh, s1mh, s2mh = tables(ROPE_DIM_M, ROPE_DIM_M, ~lo)
    ckp = jnp.where(lo[None, :], cil, cmh)
    return jnp.concatenate([ca, s1a, s2a, ci, s1i, s2i, cm, s1m, s2m, ckp, s1il, s2il, s1mh, s2mh], axis=1)


def _proj_kernel(x_ref, tab_ref, g1_ref, w_ref, gq_ref, gk_ref, gqa_ref, wqb_ref, gkva_ref, wuk_ref, wuv_ref,
                 gqn_ref, gkn_ref, gqr_ref, gkp_ref,
                 q_o, kf_o, kb_o, vf_o, vb_o, qi_o, kif_o, kpef_o, kik_o, kpek_o, iw_o,
                 qn_o, qpe_o, cf_o, kn_o, vm_o, g_o):
    def tab(i):
        return tab_ref[:, i * LANES:(i + 1) * LANES]

    x = x_ref[...]
    ms = jnp.mean(x * x, axis=-1, keepdims=True)
    h = (x * lax.rsqrt(ms + NORM_EPS) * g1_ref[...]).astype(BF16)

    lane = lax.broadcasted_iota(I32, (1, LANES), 1)
    lo = lane < IDX_DIM

    def seg_rms64(z, g):
        sq = z * z
        s_lo = jnp.sum(jnp.where(lo, sq, 0.0), axis=-1, keepdims=True)
        s_hi = jnp.sum(jnp.where(lo, 0.0, sq), axis=-1, keepdims=True)
        ms64 = jnp.where(lo, s_lo, s_hi) * (1.0 / ROPE_DIM_M)
        return z * lax.rsqrt(ms64 + NORM_EPS) * g

    z = _dot(h, w_ref[:, C_Q:C_K])
    for c in range(N_HEADS_A):
        y = _rms(z[:, c * LANES:(c + 1) * LANES], gq_ref[...])
        y = _rope3(y, tab(T_A), tab(T_A + 1), tab(T_A + 2), ROT_A // 2)
        q_o[:, c * LANES:(c + 1) * LANES] = (y * DSA_SCALE).astype(BF16)
    z = _dot(h, w_ref[:, C_K:C_V])
    for c in range(N_KV_A):
        y = _rms(z[:, c * LANES:(c + 1) * LANES], gk_ref[...])
        y = _rope3(y, tab(T_A), tab(T_A + 1), tab(T_A + 2), ROT_A // 2)
        kf_o[:, c, :] = y
        kb_o[:, c * LANES:(c + 1) * LANES] = y.astype(BF16)
    z = _dot(h, w_ref[:, C_V:C_QI])
    for c in range(N_KV_A):
        vf_o[:, c, :] = z[:, c * LANES:(c + 1) * LANES]
    vb_o[...] = z.astype(BF16)
    z = _dot(h, w_ref[:, C_QI:C_KP])
    for c in range(N_IDX_HEADS * IDX_DIM // LANES):
        y = _rope3(z[:, c * LANES:(c + 1) * LANES], tab(T_I), tab(T_I + 1), tab(T_I + 2), ROT_IDX // 2)
        qi_o[:, c * LANES:(c + 1) * LANES] = y.astype(BF16)
    z = _dot(h, w_ref[:, C_KP:C_IW])
    sq = z * z
    s_hi = jnp.sum(jnp.where(lo, 0.0, sq), axis=-1, keepdims=True)
    r_hi = lax.rsqrt(s_hi * (1.0 / ROPE_DIM_M) + NORM_EPS)
    y = jnp.where(lo, z, z * r_hi * gkp_ref[...])
    y = (y * tab(T_KP)
         + pltpu.roll(y, ROT_IDX // 2, 1) * tab(T_KP + 1)
         + pltpu.roll(y, LANES - ROT_IDX // 2, 1) * tab(T_KP + 2)
         + pltpu.roll(y, ROPE_DIM_M // 2, 1) * tab(T_KP + 3)
         + pltpu.roll(y, LANES - ROPE_DIM_M // 2, 1) * tab(T_KP + 4))
    kif_o[...] = y[:, :IDX_DIM]
    kpef_o[...] = y[:, IDX_DIM:]
    ki_lo = jnp.where(lo, y, 0.0)
    kpe_hi = jnp.where(lo, 0.0, y)
    kik_o[:, :LANES] = ki_lo.astype(BF16)
    kik_o[:, LANES:] = pltpu.roll(ki_lo, IDX_DIM, 1).astype(BF16)
    kpek_o[:, :LANES] = pltpu.roll(kpe_hi, IDX_DIM, 1).astype(BF16)
    kpek_o[:, LANES:] = kpe_hi.astype(BF16)
    z = _dot(h, w_ref[:, C_IW:C_QA])
    iw_o[...] = z * (N_IDX_HEADS ** -0.5) * (IDX_DIM ** -0.5)
    z = _dot(h, w_ref[:, C_QA:C_KVA])
    qa = _rms(z, gqa_ref[...]).astype(BF16)
    qm = _dot(qa, wqb_ref[...])
    for c in range(N_HEADS_M):
        y = _rms(qm[:, c * LANES:(c + 1) * LANES], gqn_ref[...])
        qn_o[:, c * LANES:(c + 1) * LANES] = (y * MLA_SCALE).astype(BF16)
    base = N_HEADS_M * NOPE_DIM
    for c in range(N_HEADS_M * ROPE_DIM_M // LANES):
        y = seg_rms64(qm[:, base + c * LANES:base + (c + 1) * LANES], gqr_ref[...])
        y = _rope3(y, tab(T_M), tab(T_M + 1), tab(T_M + 2), ROPE_DIM_M // 2)
        qpe_o[:, c * LANES:(c + 1) * LANES] = (y * MLA_SCALE).astype(BF16)
    z = _dot(h, w_ref[:, C_KVA:C_G])
    ckv = _rms(z, gkva_ref[...])
    cf_o[...] = ckv
    cb = ckv.astype(BF16)
    knr = _dot(cb, wuk_ref[...])
    for c in range(N_HEADS_M):
        y = _rms(knr[:, c * LANES:(c + 1) * LANES], gkn_ref[...])
        kn_o[:, c * LANES:(c + 1) * LANES] = y.astype(BF16)
    vm_o[...] = _dot(cb, wuv_ref[...]).astype(BF16)
    z = _dot(h, w_ref[:, C_G:C_END])
    g_o[...] = (1.0 / (1.0 + jnp.exp(-z))).astype(g_o.dtype)


def _project(x2d, tab, n_pos_tiles, tm, wts):
    n, d = x2d.shape
    grid = (n // tm,)

    def row(width):
        return pl.BlockSpec((tm, width), lambda i: (i, 0))

    kv3 = pl.BlockSpec((tm, N_KV_A, HEAD_DIM_A), lambda i: (i, 0, 0))
    kv3_shape = jax.ShapeDtypeStruct((n, N_KV_A, HEAD_DIM_A), F32)
    out_defs = [
        (1024, BF16), None, (512, BF16), None, (512, BF16), (512, BF16),
        (IDX_DIM, F32), (ROPE_DIM_M, F32), (256, BF16), (256, BF16), (LANES, F32),
        (1024, BF16), (512, BF16), (KV_LORA, F32), (1024, BF16), (1024, BF16), (2048, BF16),
    ]
    consts = [wts["g1"], wts["w_in"], wts["gq"], wts["gk"], wts["gqa"], wts["wqb"], wts["gkva"], wts["wuk"],
              wts["wuv"], wts["gqn"], wts["gkn"], wts["gqr"], wts["gkp"]]
    in_specs = [row(d), pl.BlockSpec((tm, N_TAB * LANES), lambda i: (i % n_pos_tiles, 0))]
    in_specs += [_const_spec(c.shape) for c in consts]
    return pl.pallas_call(
        _proj_kernel,
        grid=grid,
        in_specs=in_specs,
        out_specs=[kv3 if od is None else row(od[0]) for od in out_defs],
        out_shape=[kv3_shape if od is None else jax.ShapeDtypeStruct((n, od[0]), od[1]) for od in out_defs],
        compiler_params=pltpu.CompilerParams(dimension_semantics=("parallel",), vmem_limit_bytes=VMEM_LIMIT),
        name="project",
    )(x2d, tab, *consts)


def _topk_select(score, k, ut):
    r, n = score.shape
    nslab = n // LANES
    halves = 2 if r % 32 == 0 else 1
    rh = r // halves
    one = jnp.asarray(1, BF16)
    zero = jnp.asarray(0, BF16)
    minus1 = jnp.asarray(-1, BF16)
    ones = jnp.ones((LANES, LANES), BF16)

    bits = jnp.where(score == 0.0, 0, lax.bitcast_convert_type(score, I32))
    hi = lax.bitcast_convert_type(bits & jnp.int32(-65536), F32).astype(BF16)
    low = bits & 0xFFFF
    low = jnp.where(bits < 0, 0xFFFF - low, low)
    b1 = (low >> 8).astype(F32).astype(BF16)
    b0 = (low & 0xFF).astype(F32).astype(BF16)

    def slabs(x):
        return [[x[h * rh:(h + 1) * rh, c * LANES:(c + 1) * LANES] for c in range(nslab)] for h in range(halves)]

    def count(xs, cand, strict):
        acc = jnp.zeros((rh, LANES), BF16)
        for x in xs:
            acc = acc + jnp.where((x > cand) if strict else (x >= cand), one, zero)
        return _dot(acc, ones)

    def int_to_bf16(u):
        return u.astype(F32).astype(BF16)

    def pattern_to_bf16(u):
        raw = jnp.where(u >= 0x8000, u ^ 0x8000, u ^ 0xFFFF)
        return lax.bitcast_convert_type(lax.shift_left(raw, 16), F32).astype(BF16)

    def search(xs, kk, n_bits, to_value):
        def body(b, ts):
            bit = lax.shift_left(jnp.int32(1), n_bits - 1 - b)
            out = []
            for h in range(halves):
                cand = ts[h] | bit
                out.append(jnp.where(count(xs[h], to_value(cand), False) >= kk[h], cand, ts[h]))
            return tuple(out)

        ts = lax.fori_loop(0, n_bits, body, tuple(jnp.zeros((rh, LANES), I32) for _ in range(halves)),
                           unroll=SEARCH_UNROLL)
        return [to_value(t) for t in ts]

    hi_s, b1_s, b0_s = slabs(hi), slabs(b1), slabs(b0)
    k0 = [jnp.full((rh, LANES), k, F32) for _ in range(halves)]
    thr = search(hi_s, k0, 16, pattern_to_bf16)
    k1 = [k0[h] - count(hi_s[h], thr[h], True) for h in range(halves)]
    x1_s = [[jnp.where(hi_s[h][c] == thr[h], b1_s[h][c], minus1) for c in range(nslab)] for h in range(halves)]
    t1 = search(x1_s, k1, 8, int_to_bf16)
    k2 = [k1[h] - count(x1_s[h], t1[h], True) for h in range(halves)]
    x0_s = [[jnp.where(x1_s[h][c] == t1[h], b0_s[h][c], minus1) for c in range(nslab)] for h in range(halves)]
    t0 = search(x0_s, k2, 8, int_to_bf16)
    rows = []
    for h in range(halves):
        need = k2[h] - count(x0_s[h], t0[h], True)
        off = jnp.zeros((rh, LANES), F32)
        parts = []
        for c in range(nslab):
            above = (hi_s[h][c] > thr[h]) | (x1_s[h][c] > t1[h]) | (x0_s[h][c] > t0[h])
            eq = jnp.where(x0_s[h][c] == t0[h], one, zero)
            rank_ok = jnp.where(_dot(eq, ut) + off <= need, 1.0, 0.0).astype(BF16)
            parts.append(jnp.where(above, one, eq * rank_ok))
            off = off + _dot(eq, ones)
        rows.append(jnp.concatenate(parts, axis=1))
    return jnp.concatenate(rows, axis=0)


def _prompt_group_calls(kernel_fn, name, q_inputs, seq_inputs, const_inputs, b, s_len, qb):
    nq = s_len // qb
    outs = []
    for g in range(nq):
        n_keys = (g + 1) * qb
        in_specs = [pl.BlockSpec((None, qb, a.shape[2]), lambda bi, g=g: (bi, g, 0)) for a in q_inputs]
        in_specs += [pl.BlockSpec((None, n_keys, a.shape[2]), lambda bi: (bi, 0, 0)) for a in seq_inputs]
        in_specs += [_const_spec(a.shape) for a in const_inputs]
        args = list(q_inputs) + list(seq_inputs) + list(const_inputs)
        outs.append(pl.pallas_call(
            functools.partial(kernel_fn, g=g, qb=qb, n_keys=n_keys),
            grid=(b,),
            in_specs=in_specs,
            out_specs=pl.BlockSpec((None, qb, 1024), lambda bi: (bi, 0, 0)),
            out_shape=jax.ShapeDtypeStruct((b, qb, 1024), BF16),
            compiler_params=pltpu.CompilerParams(dimension_semantics=("parallel",), vmem_limit_bytes=VMEM_LIMIT),
            name=f"{name}_g{g}",
        )(*args))
    return outs


def _dsa_prompt_kernel(qi_ref, iw_ref, q_ref, kik_ref, kb_ref, vb_ref, ut_ref, o_ref, *, topk, g, qb, n_keys):
    qpos = g * qb + lax.broadcasted_iota(I32, (qb, 1), 0)
    kpos = lax.broadcasted_iota(I32, (1, n_keys), 1)
    valid = kpos <= qpos
    if n_keys <= topk:
        mask = valid
    else:
        iw = iw_ref[...]
        score = jnp.zeros((qb, n_keys), F32)
        for j in range(N_IDX_HEADS // 2):
            qs = qi_ref[:, j * LANES:(j + 1) * LANES]
            l0 = _dot_nt(qs, kik_ref[:, :LANES])
            l1 = _dot_nt(qs, kik_ref[:, LANES:])
            score = (score + jnp.maximum(l0, 0.0) * iw[:, 2 * j:2 * j + 1]
                     + jnp.maximum(l1, 0.0) * iw[:, 2 * j + 1:2 * j + 2])
        score = jnp.where(valid, score, NEG_FILL)
        mask = (_topk_select(score, topk, ut_ref[...]).astype(F32) > 0.5) & valid
    rep = N_HEADS_A // N_KV_A
    for kv in range(N_KV_A):
        kh = kb_ref[:, kv * LANES:(kv + 1) * LANES]
        vh = vb_ref[:, kv * LANES:(kv + 1) * LANES]
        for r in range(rep):
            hh = kv * rep + r
            s = jnp.where(mask, _dot_nt(q_ref[:, hh * LANES:(hh + 1) * LANES], kh), NEG_INF)
            m = jnp.max(s, axis=-1, keepdims=True)
            p = jnp.exp(s - m)
            l = jnp.sum(p, axis=-1, keepdims=True)
            o_ref[:, hh * LANES:(hh + 1) * LANES] = (_dot(p.astype(BF16), vh) / l).astype(o_ref.dtype)


def _dsa_prompt(qi, iw, kik, q, kb, vb, ut, b, s_len, topk, qb):
    def v3(a):
        return a.reshape(b, s_len, a.shape[-1])

    return _prompt_group_calls(functools.partial(_dsa_prompt_kernel, topk=topk), "dsa_prompt",
                               [v3(qi), v3(iw), v3(q)], [v3(kik), v3(kb), v3(vb)], [ut], b, s_len, qb)


def _mla_prompt_kernel(qn_ref, qpe_ref, kn_ref, kpek_ref, vm_ref, o_ref, *, g, qb, n_keys):
    qpos = g * qb + lax.broadcasted_iota(I32, (qb, 1), 0)
    kpos = lax.broadcasted_iota(I32, (1, n_keys), 1)
    valid = kpos <= qpos
    for h in range(N_HEADS_M):
        sl = slice(h * LANES, (h + 1) * LANES)
        qcat = jnp.concatenate([qn_ref[:, sl], qpe_ref[:, (h // 2) * LANES:(h // 2 + 1) * LANES]], axis=1)
        kcat = jnp.concatenate([kn_ref[:, sl], kpek_ref[:, (h % 2) * LANES:(h % 2 + 1) * LANES]], axis=1)
        s = jnp.where(valid, _dot_nt(qcat, kcat), NEG_INF)
        m = jnp.max(s, axis=-1, keepdims=True)
        p = jnp.exp(s - m)
        l = jnp.sum(p, axis=-1, keepdims=True)
        o_ref[:, sl] = (_dot(p.astype(BF16), vm_ref[:, sl]) / l).astype(o_ref.dtype)


def _mla_prompt(qn, qpe, kn, kpek, vm, b, s_len, qb):
    def v3(a):
        return a.reshape(b, s_len, a.shape[-1])

    return _prompt_group_calls(_mla_prompt_kernel, "mla_prompt", [v3(qn), v3(qpe)], [v3(kn), v3(kpek), v3(vm)], [],
                               b, s_len, qb)


def _merge_mlp_kernel(*refs, n_grp, ff_chunk):
    x_ref = refs[0]
    oa_refs = refs[1:1 + n_grp]
    ob_refs = refs[1 + n_grp:1 + 2 * n_grp]
    g_ref, wo_ref, g2_ref, wup_ref, wdn_ref, y_ref, m_ref = refs[1 + 2 * n_grp:]
    d = x_ref.shape[1]
    grp = pl.program_id(0) % n_grp

    def merge(j):
        ga = g_ref[:, :d].astype(F32)
        gb = g_ref[:, d:].astype(F32)
        m_ref[...] = (ga * oa_refs[j][...].astype(F32) + gb * ob_refs[j][...].astype(F32)).astype(BF16)

    for j in range(n_grp):
        pl.when(grp == j)(functools.partial(merge, j))
    x1 = x_ref[...] + _dot(m_ref[...], wo_ref[...])
    ms = jnp.mean(x1 * x1, axis=-1, keepdims=True)
    h2 = (x1 * lax.rsqrt(ms + NORM_EPS) * g2_ref[...]).astype(BF16)
    acc = x1
    for c in range(wup_ref.shape[1] // ff_chunk):
        u = jnp.maximum(_dot(h2, wup_ref[:, c * ff_chunk:(c + 1) * ff_chunk]), 0.0)
        acc = acc + _dot((u * u).astype(BF16), wdn_ref[c * ff_chunk:(c + 1) * ff_chunk, :])
    y_ref[...] = acc


def _merge_mlp(x2d, oa_list, ob_list, g, wo, g2, wup, wdn, tm):
    n, d = x2d.shape
    n_grp = len(oa_list)
    assert len(ob_list) == n_grp and all(a.shape[1:] == (tm, d) for a in oa_list + ob_list)

    def row(w):
        return pl.BlockSpec((tm, w), lambda i: (i, 0))

    grp_spec = pl.BlockSpec((None, tm, d), lambda i: (i // n_grp, 0, 0))
    return pl.pallas_call(
        functools.partial(_merge_mlp_kernel, n_grp=n_grp, ff_chunk=1024),
        grid=(n // tm,),
        in_specs=[row(d)] + [grp_spec] * (2 * n_grp) + [row(2 * d), _const_spec(wo.shape), _const_spec(g2.shape),
                                                        _const_spec(wup.shape), _const_spec(wdn.shape)],
        out_specs=row(d),
        out_shape=jax.ShapeDtypeStruct((n, d), F32),
        scratch_shapes=[pltpu.VMEM((tm, d), BF16)],
        compiler_params=pltpu.CompilerParams(dimension_semantics=("arbitrary",), vmem_limit_bytes=VMEM_LIMIT),
        name="merge_mlp",
    )(x2d, *oa_list, *ob_list, g, wo, g2, wup, wdn)


def _page_copies_t(pt_ref, cache_ref, buf_ref, sem_ref, seq, slot, n_pages, page):
    return [pltpu.make_async_copy(cache_ref.at[pt_ref[seq * n_pages + p]],
                                  buf_ref.at[slot, :, pl.ds(p * page, page)],
                                  sem_ref.at[slot]) for p in range(n_pages)]


def _samp_score_kernel(pt_ref, qi_ref, wc_ref, kin_ref, cache_ref, o_ref, buf_ref, sem_ref, *, n_pages, page, chunk):
    s = pl.program_id(0)
    ns = pl.num_programs(0)
    slot = s % 2
    past = n_pages * page

    @pl.when(s == 0)
    def _():
        for cp in _page_copies_t(pt_ref, cache_ref, buf_ref, sem_ref, 0, 0, n_pages, page):
            cp.start()

    @pl.when(s + 1 < ns)
    def _():
        for cp in _page_copies_t(pt_ref, cache_ref, buf_ref, sem_ref, s + 1, 1 - slot, n_pages, page):
            cp.start()

    pltpu.make_async_copy(buf_ref.at[1 - slot], buf_ref.at[slot], sem_ref.at[slot]).wait()

    qi = qi_ref[0]
    wc = wc_ref[0]

    def score_of(keys_t):
        l = _dot(qi, keys_t.astype(BF16))
        return jnp.sum(jnp.maximum(l, 0.0) * wc, axis=0, keepdims=True)

    for c in range(past // chunk):
        o_ref[0, :, c * chunk:(c + 1) * chunk] = score_of(buf_ref[slot, :, c * chunk:(c + 1) * chunk])
    lane = lax.broadcasted_iota(I32, (1, LANES), 1)
    tail_keys = jnp.where(lane == 0, kin_ref[0], 0.0)
    o_ref[0, :, past:] = jnp.where(lane == 0, score_of(tail_keys), NEG_FILL)


def _samp_scores(pt_flat, qi3, wc3, kin3, cache_idx_t, n_pages, page):
    ns = qi3.shape[0]
    past = n_pages * page
    chunk = min(2048, past)
    grid_spec = pltpu.PrefetchScalarGridSpec(
        num_scalar_prefetch=1,
        grid=(ns,),
        in_specs=[pl.BlockSpec((1, N_IDX_HEADS, IDX_DIM), lambda s, pt: (s, 0, 0)),
                  pl.BlockSpec((1, N_IDX_HEADS, 1), lambda s, pt: (s, 0, 0)),
                  pl.BlockSpec((1, IDX_DIM, 1), lambda s, pt: (s, 0, 0)),
                  pl.BlockSpec(memory_space=pl.ANY)],
        out_specs=pl.BlockSpec((1, 1, past + LANES), lambda s, pt: (s, 0, 0)),
        scratch_shapes=[pltpu.VMEM((2, IDX_DIM, past), F32), pltpu.SemaphoreType.DMA((2,))],
    )
    return pl.pallas_call(
        functools.partial(_samp_score_kernel, n_pages=n_pages, page=page, chunk=chunk),
        grid_spec=grid_spec,
        out_shape=jax.ShapeDtypeStruct((ns, 1, past + LANES), F32),
        compiler_params=pltpu.CompilerParams(dimension_semantics=("arbitrary",), vmem_limit_bytes=VMEM_LIMIT),
        name="sample_scores",
    )(pt_flat, qi3, wc3, kin3, cache_idx_t)


def _samp_select_kernel(score_ref, ut_ref, j8_ref, idx_ref, np_ref, rank_ref, *, topk, past, chunk):
    ns = score_ref.shape[0]
    sel = _topk_select(score_ref[...], topk, ut_ref[...])
    off = jnp.zeros((ns, 1), F32)
    for c in range(past // LANES):
        sl = slice(c * LANES, (c + 1) * LANES)
        sc = sel[:, sl]
        pc = _dot(sc, ut_ref[...])
        rank_ref[:, sl] = jnp.where(sc.astype(F32) > 0.5, pc + off, 0.0)
        off = off + pc[:, LANES - 1:LANES]
    np_ref[...] = jnp.broadcast_to(off, np_ref.shape).astype(I32)
    want = (lax.broadcasted_iota(I32, (topk, 1), 0) + 1).astype(F32)

    def per_seq(s, carry):
        acc = jnp.zeros((8, topk), F32)
        for c in range(past // chunk):
            rr = rank_ref[pl.ds(s, 1), c * chunk:(c + 1) * chunk]
            onehot = jnp.where(rr == want, 1.0, 0.0).astype(BF16)
            acc = acc + _dot_nt(j8_ref[:, c * chunk:(c + 1) * chunk], onehot)
        idx_ref[pl.ds(s, 1), :] = (acc[0:1, :] * float(LANES) + acc[1:2, :]).astype(I32)
        return carry

    lax.fori_loop(0, ns, per_seq, 0, unroll=2)


def _samp_select(score2d, ut, j8, topk, past):
    ns, n = score2d.shape
    chunk = min(1024, past)
    return pl.pallas_call(
        functools.partial(_samp_select_kernel, topk=topk, past=past, chunk=chunk),
        grid=(1,),
        in_specs=[pl.BlockSpec((ns, n), lambda i: (0, 0)), pl.BlockSpec(ut.shape, lambda i: (0, 0)),
                  pl.BlockSpec(j8.shape, lambda i: (0, 0))],
        out_specs=[pl.BlockSpec((ns, topk), lambda i: (0, 0)), pl.BlockSpec((ns, LANES), lambda i: (0, 0))],
        out_shape=[jax.ShapeDtypeStruct((ns, topk), I32), jax.ShapeDtypeStruct((ns, LANES), I32)],
        scratch_shapes=[pltpu.VMEM((ns, past), F32)],
        compiler_params=pltpu.CompilerParams(dimension_semantics=("arbitrary",), vmem_limit_bytes=VMEM_LIMIT),
        name="sample_select",
    )(score2d, ut, j8)


def _dsa_rows_attend(q_ref, kn_ref, vn_ref, kbuf, vbuf, o_ref, n_past, topk):
    slot_ok = lax.broadcasted_iota(I32, (1, topk), 1) < n_past
    new_ok = n_past < topk
    rep = N_HEADS_A // N_KV_A
    for kv in range(N_KV_A):
        kh = kbuf[:, kv, :].astype(BF16)
        vh = vbuf[:, kv, :].astype(BF16)
        q2 = q_ref[0, kv * rep:(kv + 1) * rep, :]
        k_new = kn_ref[0, kv:kv + 1, :].astype(BF16).astype(F32)
        v_new = vn_ref[0, kv:kv + 1, :].astype(BF16).astype(F32)
        sc = jnp.where(slot_ok, _dot_nt(q2, kh), NEG_INF)
        s_new = jnp.sum(q2.astype(F32) * k_new, axis=-1, keepdims=True)
        s_new = jnp.where(new_ok, s_new, NEG_INF)
        m = jnp.maximum(jnp.max(sc, axis=-1, keepdims=True), s_new)
        p = jnp.exp(sc - m)
        p_new = jnp.exp(s_new - m)
        l = jnp.sum(p, axis=-1, keepdims=True) + p_new
        o = (_dot(p.astype(BF16), vh) + p_new.astype(BF16).astype(F32) * v_new) / l
        for g in range(rep):
            hh = kv * rep + g
            o_ref[0, :, hh * LANES:(hh + 1) * LANES] = o[g:g + 1, :]


def _page_copies(pt_ref, cache_ref, buf_ref, sem_ref, seq, slot, n_pages, page):
    return [pltpu.make_async_copy(cache_ref.at[pt_ref[seq * n_pages + p]],
                                  buf_ref.at[slot, pl.ds(p * page, page)],
                                  sem_ref.at[slot]) for p in range(n_pages)]


def _samp_attn_kernel(pt_ref, idx_ref, np_ref, qn_ref, qcol_ref, gcol_ref, qpe_ref, cn_ref, knn_ref, kpn_ref,
                      wukt_ref, wuktf_ref, wuv_ref, qa_ref, kna_ref, vna_ref, cc_ref, cp_ref, ck_ref, cv_ref,
                      o_ref, oa_ref, cbuf, pbuf, s_ref, kbuf, vbuf, sem_ref, row_sem, *, n_pages, page, chunk, topk):
    s = pl.program_id(0)
    ns = pl.num_programs(0)
    slot = s % 2
    past = n_pages * page
    n_chunks = past // chunk
    rows_per_chunk = topk // n_chunks

    def copies(seq, sl):
        return (_page_copies(pt_ref, cc_ref, cbuf, sem_ref.at[0], seq, sl, n_pages, page)
                + _page_copies_t(pt_ref, cp_ref, pbuf, sem_ref.at[1], seq, sl, n_pages, page))

    @pl.when(s == 0)
    def _():
        for cp in copies(0, 0):
            cp.start()

    @pl.when(s + 1 < ns)
    def _():
        for cp in copies(s + 1, 1 - slot):
            cp.start()

    pltpu.make_async_copy(cbuf.at[1 - slot], cbuf.at[slot], sem_ref.at[0, slot]).wait()
    pltpu.make_async_copy(pbuf.at[1 - slot], pbuf.at[slot], sem_ref.at[1, slot]).wait()

    qg = qcol_ref[0].astype(F32) * gcol_ref[...]
    prod = wuktf_ref[...] * qg
    q_abs = jnp.concatenate([jnp.sum(prod[h * LANES:(h + 1) * LANES, :], axis=0, keepdims=True)
                             for h in range(N_HEADS_M)], axis=0).astype(BF16)
    qpe = qpe_ref[0]

    def score_chunk(c, carry):
        for r in range(rows_per_chunk):
            row = c * rows_per_chunk + r
            pos = idx_ref[s * topk + row]
            pg = pt_ref[s * n_pages + pos // page]
            off = pos % page
            pltpu.make_async_copy(ck_ref.at[pg, off], kbuf.at[row], row_sem.at[0]).start()
            pltpu.make_async_copy(cv_ref.at[pg, off], vbuf.at[row], row_sem.at[1]).start()
        start = pl.multiple_of(c * chunk, chunk)
        cb = cbuf[slot, pl.ds(start, chunk), :].astype(BF16)
        knt = _dot_nt(wukt_ref[...], cb)
        ssq = jnp.concatenate([jnp.sum(jnp.square(knt[h * LANES:(h + 1) * LANES, :]), axis=0, keepdims=True)
                               for h in range(N_HEADS_M)], axis=0)
        s_nope = _dot_nt(q_abs, cb) * lax.rsqrt(ssq * (1.0 / NOPE_DIM) + NORM_EPS)
        s_pe = _dot(qpe, pbuf[slot, :, pl.ds(start, chunk)].astype(BF16))
        s_ref[:, pl.ds(start, chunk)] = s_nope + s_pe
        return carry

    lax.fori_loop(0, n_chunks, score_chunk, 0)
    qn = qn_ref[0].astype(F32)
    s_new = (jnp.sum(qn * knn_ref[0].astype(F32), axis=-1, keepdims=True)
             + jnp.sum(qpe.astype(F32) * kpn_ref[0].astype(BF16).astype(F32), axis=-1, keepdims=True))
    lane = lax.broadcasted_iota(I32, (1, LANES), 1)
    s_ref[:, past:] = jnp.where(lane == 0, s_new, NEG_INF)

    sc = s_ref[...]
    m = jnp.max(sc, axis=-1, keepdims=True)
    p = jnp.exp(sc - m)
    l = jnp.sum(p, axis=-1, keepdims=True)
    s_ref[...] = p
    p_new = jnp.exp(s_new - m)

    def pv_chunk(c, acc):
        start = pl.multiple_of(c * chunk, chunk)
        cb = cbuf[slot, pl.ds(start, chunk), :].astype(BF16)
        return acc + _dot(s_ref[:, pl.ds(start, chunk)].astype(BF16), cb)

    o_lat = lax.fori_loop(0, n_chunks, pv_chunk, jnp.zeros((N_HEADS_M, KV_LORA), F32))
    o_lat = (o_lat + p_new.astype(BF16).astype(F32) * cn_ref[0].astype(BF16).astype(F32)) / l
    o8 = _dot(o_lat.astype(BF16), wuv_ref[...])
    for h in range(N_HEADS_M):
        o_ref[0, :, h * LANES:(h + 1) * LANES] = o8[h:h + 1, h * LANES:(h + 1) * LANES]

    pltpu.make_async_copy(vbuf, kbuf, row_sem.at[0]).wait()
    pltpu.make_async_copy(kbuf, vbuf, row_sem.at[1]).wait()
    _dsa_rows_attend(qa_ref, kna_ref, vna_ref, kbuf, vbuf, oa_ref, np_ref[s], topk)


def _samp_attn(pt_flat, idx_flat, n_past, qn3, qcol, gcol, qpe3, cn3, knn3, kpn3, wukt, wukt_f32, wuv,
               qa3, kna3, vna3, cache_ckv, cache_kpe_t, cache_k, cache_v, n_pages, page, topk):
    ns = qn3.shape[0]
    past = n_pages * page
    chunk = min(1024, past)
    assert topk % (past // chunk) == 0

    def per_seq(shape):
        return pl.BlockSpec((1,) + shape, lambda s, *_: (s, 0, 0))

    def const(shape):
        nd = len(shape)
        return pl.BlockSpec(shape, lambda s, *_: (0,) * nd, pipeline_mode=pl.Buffered(1))

    hbm = pl.BlockSpec(memory_space=pl.ANY)
    grid_spec = pltpu.PrefetchScalarGridSpec(
        num_scalar_prefetch=3,
        grid=(ns,),
        in_specs=[per_seq((N_HEADS_M, NOPE_DIM)), per_seq((N_HEADS_M * NOPE_DIM, 1)), const(gcol.shape),
                  per_seq((N_HEADS_M, ROPE_DIM_M)), per_seq((1, KV_LORA)), per_seq((N_HEADS_M, NOPE_DIM)),
                  per_seq((1, ROPE_DIM_M)), const(wukt.shape), const(wukt_f32.shape), const(wuv.shape),
                  per_seq((N_HEADS_A, HEAD_DIM_A)), per_seq((N_KV_A, HEAD_DIM_A)), per_seq((N_KV_A, HEAD_DIM_A)),
                  hbm, hbm, hbm, hbm],
        out_specs=[per_seq((1, N_HEADS_M * V_DIM_M)), per_seq((1, N_HEADS_A * HEAD_DIM_A))],
        scratch_shapes=[pltpu.VMEM((2, past, KV_LORA), F32), pltpu.VMEM((2, ROPE_DIM_M, past), F32),
                        pltpu.VMEM((N_HEADS_M, past + LANES), F32),
                        pltpu.VMEM((topk, N_KV_A, HEAD_DIM_A), F32), pltpu.VMEM((topk, N_KV_A, HEAD_DIM_A), F32),
                        pltpu.SemaphoreType.DMA((2, 2)), pltpu.SemaphoreType.DMA((2,))],
    )
    out_row = jax.ShapeDtypeStruct((ns, 1, N_HEADS_M * V_DIM_M), F32)
    return pl.pallas_call(
        functools.partial(_samp_attn_kernel, n_pages=n_pages, page=page, chunk=chunk, topk=topk),
        grid_spec=grid_spec,
        out_shape=[out_row, out_row],
        compiler_params=pltpu.CompilerParams(dimension_semantics=("arbitrary",), vmem_limit_bytes=VMEM_LIMIT),
        name="sample_attn",
    )(pt_flat, idx_flat, n_past, qn3, qcol, gcol, qpe3, cn3, knn3, kpn3, wukt, wukt_f32, wuv,
      qa3, kna3, vna3, cache_ckv, cache_kpe_t, cache_k, cache_v)


def _prep_weights(norm1_g, w_in, a_qn_g, a_kn_g, q_a_norm_g, w_q_b, kv_a_norm_g, w_uk, w_uv,
                  m_qn_g, m_kn_g, m_qr_g, m_kr_g):
    d = w_in.shape[0]
    sizes = (1024, 512, 512, 512, IDX_DIM, N_IDX_HEADS, Q_LORA, KV_LORA, ROPE_DIM_M, 2 * d)
    offs = [0]
    for sz in sizes:
        offs.append(offs[-1] + sz)
    wq, wk, wv, wqi, wki, wiw, wqa, wkva, wpe, wg = [w_in[:, offs[i]:offs[i + 1]] for i in range(len(sizes))]
    w_perm = jnp.concatenate(
        [wq, wk, wv, wqi, wki, wpe, wiw, jnp.zeros((d, LANES - N_IDX_HEADS), w_in.dtype), wqa, wkva, wg],
        axis=1).astype(BF16)
    wqb = jnp.concatenate([w_q_b[:, :, :NOPE_DIM].reshape(Q_LORA, -1), w_q_b[:, :, NOPE_DIM:].reshape(Q_LORA, -1)],
                          axis=1).astype(BF16)

    def rowv(v):
        return v.astype(F32).reshape(1, -1)

    return {
        "g1": rowv(norm1_g), "w_in": w_perm, "gq": rowv(a_qn_g), "gk": rowv(a_kn_g), "gqa": rowv(q_a_norm_g),
        "wqb": wqb, "gkva": rowv(kv_a_norm_g), "wuk": w_uk.reshape(KV_LORA, -1).astype(BF16),
        "wuv": w_uv.reshape(KV_LORA, -1).astype(BF16),
        "gqn": rowv(m_qn_g), "gkn": rowv(m_kn_g), "gqr": rowv(jnp.tile(m_qr_g, 2)),
        "gkp": rowv(jnp.concatenate([jnp.ones((IDX_DIM,), F32), m_kr_g.astype(F32)])),
    }


def kernel(x_prompt, x_sample, cache_k, cache_v, cache_idx_k, cache_ckv, cache_kpe, page_table,
           norm1_g, w_in, a_qn_g, a_kn_g, q_a_norm_g, w_q_b, kv_a_norm_g, w_uk, w_uv,
           m_qn_g, m_kn_g, m_qr_g, m_kr_g, w_o, norm2_g, w_up, w_down):
    b, s_len, d = x_prompt.shape
    ns, t_s, _ = x_sample.shape
    page = cache_k.shape[1]
    n_pages = page_table.shape[1]
    past = n_pages * page
    assert d == N_HEADS_A * HEAD_DIM_A and t_s == 1
    assert w_in.shape[1] == C_END - (LANES - N_IDX_HEADS)

    wts = _prep_weights(norm1_g, w_in, a_qn_g, a_kn_g, q_a_norm_g, w_q_b, kv_a_norm_g, w_uk, w_uv,
                        m_qn_g, m_kn_g, m_qr_g, m_kr_g)
    wo = w_o.astype(BF16)
    wup = w_up.astype(BF16)
    wdn = w_down.astype(BF16)
    g2 = norm2_g.astype(F32).reshape(1, -1)
    ut = (jnp.arange(LANES)[:, None] <= jnp.arange(LANES)[None, :]).astype(BF16)

    tm = min(256, s_len)
    qb = min(PROMPT_Q_BLOCK, s_len)
    tab_p = _rope_tables(jnp.arange(s_len))
    xp = x_prompt.reshape(b * s_len, d)
    (q, kf, kb, vf, vb, qi, kif, kpef, kik, kpek, iw, qn, qpe, cf, kn, vm, g) = _project(xp, tab_p, s_len // tm, tm, wts)
    topk_p = min(TOPK_MAX, s_len // 4)
    oa = _dsa_prompt(qi, iw, kik, q, kb, vb, ut, b, s_len, topk_p, qb)
    ob = _mla_prompt(qn, qpe, kn, kpek, vm, b, s_len, qb)
    y_prompt = _merge_mlp(xp, oa, ob, g, wo, g2, wup, wdn, qb).reshape(b, s_len, d)

    tab_s = _rope_tables(jnp.full((ns,), past, I32))
    xs = x_sample.reshape(ns, d)
    (sq, skf, _skb, svf, _svb, sqi, skif, skpef, _skik, _skpek, siw, sqn, sqpe, scf, skn, _svm, sg) = _project(
        xs, tab_s, 1, ns, wts)
    topk_s = min(TOPK_MAX, (past + 1) // 4)
    pt_flat = page_table.reshape(-1).astype(I32)
    cache_idx_t = jnp.swapaxes(cache_idx_k, 1, 2)
    cache_kpe_t = jnp.swapaxes(cache_kpe, 1, 2)
    score = _samp_scores(pt_flat, sqi.reshape(ns, N_IDX_HEADS, IDX_DIM), siw[:, :N_IDX_HEADS].reshape(ns, N_IDX_HEADS, 1),
                         skif.reshape(ns, IDX_DIM, 1), cache_idx_t, n_pages, page)
    pos = jnp.arange(past)
    j8 = jnp.zeros((8, past), F32).at[0].set(pos // LANES).at[1].set(pos % LANES).astype(BF16)
    idx, n_past = _samp_select(score.reshape(ns, past + LANES), ut, j8, topk_s, past)
    gcol = jnp.tile(m_kn_g.astype(F32), N_HEADS_M).reshape(-1, 1)
    wukt_f32 = w_uk.reshape(KV_LORA, -1).T.astype(F32)
    ob_s, oa_s = _samp_attn(pt_flat, idx.reshape(-1), n_past[:, 0],
                            sqn.reshape(ns, N_HEADS_M, NOPE_DIM), sqn.reshape(ns, N_HEADS_M * NOPE_DIM, 1), gcol,
                            sqpe.reshape(ns, N_HEADS_M, ROPE_DIM_M), scf.reshape(ns, 1, KV_LORA),
                            skn.reshape(ns, N_HEADS_M, NOPE_DIM), skpef.reshape(ns, 1, ROPE_DIM_M),
                            wukt_f32.astype(BF16), wukt_f32, wts["wuv"],
                            sq.reshape(ns, N_HEADS_A, HEAD_DIM_A), skf, svf,
                            cache_ckv, cache_kpe_t, cache_k, cache_v, n_pages, page, topk_s)
    y_sample = _merge_mlp(xs, [oa_s.reshape(1, ns, d)], [ob_s.reshape(1, ns, d)], sg, wo, g2, wup, wdn,
                          ns).reshape(ns, 1, d)

    return (y_prompt, y_sample,
            kf.reshape(b, s_len, N_KV_A, HEAD_DIM_A), vf.reshape(b, s_len, N_KV_A, HEAD_DIM_A),
            kif.reshape(b, s_len, IDX_DIM), cf.reshape(b, s_len, KV_LORA), kpef.reshape(b, s_len, ROPE_DIM_M),
            skf.reshape(ns, 1, N_KV_A, HEAD_DIM_A), svf.reshape(ns, 1, N_KV_A, HEAD_DIM_A),
            skif.reshape(ns, 1, IDX_DIM), scf.reshape(ns, 1, KV_LORA), skpef.reshape(ns, 1, ROPE_DIM_M))
```

```python
import functools

import jax
import jax.numpy as jnp
from jax import lax
from jax.experimental import pallas as pl
from jax.experimental.pallas import tpu as pltpu

F32 = jnp.float32
BF16 = jnp.bfloat16
I32 = jnp.int32

N_HEADS_A = 8
HEAD_DIM_A = 128
N_KV_A = 4
ROT_A = 32
N_IDX_HEADS = 8
IDX_DIM = 64
ROT_IDX = 16
TOPK_MAX = 256
N_HEADS_M = 8
Q_LORA = 256
KV_LORA = 256
NOPE_DIM = 128
ROPE_DIM_M = 64
V_DIM_M = 128
DSA_SCALE = HEAD_DIM_A ** -0.5
MLA_SCALE = (NOPE_DIM + ROPE_DIM_M) ** -0.5
ROPE_THETA = 500000.0
NORM_EPS = 1e-6
LANES = 128
NEG_INF = float("-inf")
NEG_FILL = -3.0e38

C_Q, C_K, C_V, C_QI, C_KP, C_IW, C_QA, C_KVA, C_G, C_END = 0, 1024, 1536, 2048, 2560, 2688, 2816, 3072, 3328, 5376

T_A, T_I, T_M, T_KP, N_TAB = 0, 3, 6, 9, 14

VMEM_LIMIT = 56 * 1024 * 1024
PROMPT_Q_BLOCK = 256
SEARCH_UNROLL = 4


def _dot(a, b):
    return jnp.dot(a, b, preferred_element_type=F32)


def _dot_nt(a, b):
    return lax.dot_general(a, b, (((1,), (1,)), ((), ())), preferred_element_type=F32)


def _rms(z, g):
    ms = jnp.mean(z * z, axis=-1, keepdims=True)
    return z * lax.rsqrt(ms + NORM_EPS) * g


def _rope3(y, c, s1, s2, shift):
    return y * c + pltpu.roll(y, shift, 1) * s1 + pltpu.roll(y, LANES - shift, 1) * s2


def _const_spec(shape):
    nd = len(shape)
    return pl.BlockSpec(shape, lambda *_: (0,) * nd, pipeline_mode=pl.Buffered(1))


def _rope_tables(pos):
    pos = pos.astype(F32)
    lane = jnp.arange(LANES)

    def tables(rot, period, lanes_sel=None):
        half = rot // 2
        inv = jnp.power(jnp.float32(ROPE_THETA), -jnp.arange(half, dtype=F32) * (2.0 / rot))
        ang = pos[:, None] * inv[None, :]
        cos, sin = jnp.cos(ang), jnp.sin(ang)
        lp = lane % period
        in_rot = lp < rot
        first = lp < half
        second = in_rot & ~first
        cidx = lp % half
        c = jnp.where(in_rot[None, :], cos[:, cidx], 1.0)
        s1 = jnp.where(second[None, :], sin[:, cidx], 0.0)
        s2 = jnp.where(first[None, :], -sin[:, cidx], 0.0)
        if lanes_sel is not None:
            c = jnp.where(lanes_sel[None, :], c, 1.0)
            s1 = jnp.where(lanes_sel[None, :], s1, 0.0)
            s2 = jnp.where(lanes_sel[None, :], s2, 0.0)
        return c, s1, s2

    ca, s1a, s2a = tables(ROT_A, HEAD_DIM_A)
    ci, s1i, s2i = tables(ROT_IDX, IDX_DIM)
    cm, s1m, s2m = tables(ROPE_DIM_M, ROPE_DIM_M)
    lo = lane < IDX_DIM
    cil, s1il, s2il = tables(ROT_IDX, IDX_DIM, lo)
    cmh, s1mh, s2mh = tables(ROPE_DIM_M, ROPE_DIM_M, ~lo)
    ckp = jnp.where(lo[None, :], cil, cmh)
    return jnp.concatenate([ca, s1a, s2a, ci, s1i, s2i, cm, s1m, s2m, ckp, s1il, s2il, s1mh, s2mh], axis=1)


def _proj_kernel(x_ref, tab_ref, g1_ref, w_ref, gq_ref, gk_ref, gqa_ref, wqb_ref, gkva_ref, wuk_ref, wuv_ref,
                 gqn_ref, gkn_ref, gqr_ref, gkp_ref,
                 q_o, kf_o, kb_o, vf_o, vb_o, qi_o, kif_o, kpef_o, kik_o, kpek_o, iw_o,
                 qn_o, qpe_o, cf_o, kn_o, vm_o, g_o):
    def tab(i):
        return tab_ref[:, i * LANES:(i + 1) * LANES]

    x = x_ref[...]
    ms = jnp.mean(x * x, axis=-1, keepdims=True)
    h = (x * lax.rsqrt(ms + NORM_EPS) * g1_ref[...]).astype(BF16)

    lane = lax.broadcasted_iota(I32, (1, LANES), 1)
    lo = lane < IDX_DIM

    def seg_rms64(z, g):
        sq = z * z
        s_lo = jnp.sum(jnp.where(lo, sq, 0.0), axis=-1, keepdims=True)
        s_hi = jnp.sum(jnp.where(lo, 0.0, sq), axis=-1, keepdims=True)
        ms64 = jnp.where(lo, s_lo, s_hi) * (1.0 / ROPE_DIM_M)
        return z * lax.rsqrt(ms64 + NORM_EPS) * g

    z = _dot(h, w_ref[:, C_Q:C_K])
    for c in range(N_HEADS_A):
        y = _rms(z[:, c * LANES:(c + 1) * LANES], gq_ref[...])
        y = _rope3(y, tab(T_A), tab(T_A + 1), tab(T_A + 2), ROT_A // 2)
        q_o[:, c * LANES:(c + 1) * LANES] = (y * DSA_SCALE).astype(BF16)
    z = _dot(h, w_ref[:, C_K:C_V])
    for c in range(N_KV_A):
        y = _rms(z[:, c * LANES:(c + 1) * LANES], gk_ref[...])
        y = _rope3(y, tab(T_A), tab(T_A + 1), tab(T_A + 2), ROT_A // 2)
        kf_o[:, c, :] = y
        kb_o[:, c * LANES:(c + 1) * LANES] = y.astype(BF16)
    z = _dot(h, w_ref[:, C_V:C_QI])
    for c in range(N_KV_A):
        vf_o[:, c, :] = z[:, c * LANES:(c + 1) * LANES]
    ones_blk = jnp.ones((z.shape[0], LANES), BF16)
    for c in range(N_KV_A):
        vb_o[:, 2 * c * LANES:(2 * c + 1) * LANES] = z[:, c * LANES:(c + 1) * LANES].astype(BF16)
        vb_o[:, (2 * c + 1) * LANES:(2 * c + 2) * LANES] = ones_blk
    z = _dot(h, w_ref[:, C_QI:C_KP])
    for c in range(N_IDX_HEADS * IDX_DIM // LANES):
        y = _rope3(z[:, c * LANES:(c + 1) * LANES], tab(T_I), tab(T_I + 1), tab(T_I + 2), ROT_IDX // 2)
        qi_o[:, c * LANES:(c + 1) * LANES] = y.astype(BF16)
    z = _dot(h, w_ref[:, C_KP:C_IW])
    sq = z * z
    s_hi = jnp.sum(jnp.where(lo, 0.0, sq), axis=-1, keepdims=True)
    r_hi = lax.rsqrt(s_hi * (1.0 / ROPE_DIM_M) + NORM_EPS)
    y = jnp.where(lo, z, z * r_hi * gkp_ref[...])
    y = (y * tab(T_KP)
         + pltpu.roll(y, ROT_IDX // 2, 1) * tab(T_KP + 1)
         + pltpu.roll(y, LANES - ROT_IDX // 2, 1) * tab(T_KP + 2)
         + pltpu.roll(y, ROPE_DIM_M // 2, 1) * tab(T_KP + 3)
         + pltpu.roll(y, LANES - ROPE_DIM_M // 2, 1) * tab(T_KP + 4))
    kif_o[...] = y[:, :IDX_DIM]
    kpef_o[...] = y[:, IDX_DIM:]
    ki_lo = jnp.where(lo, y, 0.0)
    kpe_hi = jnp.where(lo, 0.0, y)
    kik_o[:, :LANES] = ki_lo.astype(BF16)
    kik_o[:, LANES:] = pltpu.roll(ki_lo, IDX_DIM, 1).astype(BF16)
    kpek_o[:, :LANES] = pltpu.roll(kpe_hi, IDX_DIM, 1).astype(BF16)
    kpek_o[:, LANES:] = kpe_hi.astype(BF16)
    z = _dot(h, w_ref[:, C_IW:C_QA])
    iw_o[...] = z * (N_IDX_HEADS ** -0.5) * (IDX_DIM ** -0.5)
    z = _dot(h, w_ref[:, C_QA:C_KVA])
    qa = _rms(z, gqa_ref[...]).astype(BF16)
    qm = _dot(qa, wqb_ref[...])
    for c in range(N_HEADS_M):
        y = _rms(qm[:, c * LANES:(c + 1) * LANES], gqn_ref[...])
        qn_o[:, c * LANES:(c + 1) * LANES] = (y * MLA_SCALE).astype(BF16)
    base = N_HEADS_M * NOPE_DIM
    for c in range(N_HEADS_M * ROPE_DIM_M // LANES):
        y = seg_rms64(qm[:, base + c * LANES:base + (c + 1) * LANES], gqr_ref[...])
        y = _rope3(y, tab(T_M), tab(T_M + 1), tab(T_M + 2), ROPE_DIM_M // 2)
        qpe_o[:, c * LANES:(c + 1) * LANES] = (y * MLA_SCALE).astype(BF16)
    z = _dot(h, w_ref[:, C_KVA:C_G])
    ckv = _rms(z, gkva_ref[...])
    cf_o[...] = ckv
    cb = ckv.astype(BF16)
    knr = _dot(cb, wuk_ref[...])
    for c in range(N_HEADS_M):
        y = _rms(knr[:, c * LANES:(c + 1) * LANES], gkn_ref[...])
        kn_o[:, c * LANES:(c + 1) * LANES] = y.astype(BF16)
    vmz = _dot(cb, wuv_ref[...])
    for c in range(N_HEADS_M):
        vm_o[:, 2 * c * LANES:(2 * c + 1) * LANES] = vmz[:, c * LANES:(c + 1) * LANES].astype(BF16)
        vm_o[:, (2 * c + 1) * LANES:(2 * c + 2) * LANES] = ones_blk
    z = _dot(h, w_ref[:, C_G:C_END])
    g_o[...] = (1.0 / (1.0 + jnp.exp(-z))).astype(g_o.dtype)


def _project(x2d, tab, n_pos_tiles, tm, wts):
    n, d = x2d.shape
    grid = (n // tm,)

    def row(width):
        return pl.BlockSpec((tm, width), lambda i: (i, 0))

    kv3 = pl.BlockSpec((tm, N_KV_A, HEAD_DIM_A), lambda i: (i, 0, 0))
    kv3_shape = jax.ShapeDtypeStruct((n, N_KV_A, HEAD_DIM_A), F32)
    out_defs = [
        (1024, BF16), None, (512, BF16), None, (1024, BF16), (512, BF16),
        (IDX_DIM, F32), (ROPE_DIM_M, F32), (256, BF16), (256, BF16), (LANES, F32),
        (1024, BF16), (512, BF16), (KV_LORA, F32), (1024, BF16), (2048, BF16), (2048, BF16),
    ]
    consts = [wts["g1"], wts["w_in"], wts["gq"], wts["gk"], wts["gqa"], wts["wqb"], wts["gkva"], wts["wuk"],
              wts["wuv"], wts["gqn"], wts["gkn"], wts["gqr"], wts["gkp"]]
    in_specs = [row(d), pl.BlockSpec((tm, N_TAB * LANES), lambda i: (i % n_pos_tiles, 0))]
    in_specs += [_const_spec(c.shape) for c in consts]
    return pl.pallas_call(
        _proj_kernel,
        grid=grid,
        in_specs=in_specs,
        out_specs=[kv3 if od is None else row(od[0]) for od in out_defs],
        out_shape=[kv3_shape if od is None else jax.ShapeDtypeStruct((n, od[0]), od[1]) for od in out_defs],
        compiler_params=pltpu.CompilerParams(dimension_semantics=("parallel",), vmem_limit_bytes=VMEM_LIMIT),
        name="project",
    )(x2d, tab, *consts)


def _topk_select(score, k, ut):
    r, n = score.shape
    nslab = n // LANES
    halves = 2 if r % 32 == 0 else 1
    rh = r // halves
    one = jnp.asarray(1, BF16)
    zero = jnp.asarray(0, BF16)
    minus1 = jnp.asarray(-1, BF16)
    ones = jnp.ones((LANES, LANES), BF16)

    bits = jnp.where(score == 0.0, 0, lax.bitcast_convert_type(score, I32))
    hi = lax.bitcast_convert_type(bits & jnp.int32(-65536), F32).astype(BF16)
    low = bits & 0xFFFF
    low = jnp.where(bits < 0, 0xFFFF - low, low)
    b1 = (low >> 8).astype(F32).astype(BF16)
    b0 = (low & 0xFF).astype(F32).astype(BF16)

    def slabs(x):
        return [[x[h * rh:(h + 1) * rh, c * LANES:(c + 1) * LANES] for c in range(nslab)] for h in range(halves)]

    def count(xs, cand, strict):
        acc = jnp.zeros((rh, LANES), BF16)
        for x in xs:
            acc = acc + jnp.where((x > cand) if strict else (x >= cand), one, zero)
        return _dot(acc, ones)

    def int_to_bf16(u):
        return u.astype(F32).astype(BF16)

    def pattern_to_bf16(u):
        raw = jnp.where(u >= 0x8000, u ^ 0x8000, u ^ 0xFFFF)
        return lax.bitcast_convert_type(lax.shift_left(raw, 16), F32).astype(BF16)

    def search(xs, kk, n_bits, to_value):
        def body(b, ts):
            bit = lax.shift_left(jnp.int32(1), n_bits - 1 - b)
            out = []
            for h in range(halves):
                cand = ts[h] | bit
                out.append(jnp.where(count(xs[h], to_value(cand), False) >= kk[h], cand, ts[h]))
            return tuple(out)

        ts = lax.fori_loop(0, n_bits, body, tuple(jnp.zeros((rh, LANES), I32) for _ in range(halves)),
                           unroll=SEARCH_UNROLL)
        return [to_value(t) for t in ts]

    hi_s, b1_s, b0_s = slabs(hi), slabs(b1), slabs(b0)
    k0 = [jnp.full((rh, LANES), k, F32) for _ in range(halves)]
    thr = search(hi_s, k0, 16, pattern_to_bf16)
    k1 = [k0[h] - count(hi_s[h], thr[h], True) for h in range(halves)]
    x1_s = [[jnp.where(hi_s[h][c] == thr[h], b1_s[h][c], minus1) for c in range(nslab)] for h in range(halves)]
    t1 = search(x1_s, k1, 8, int_to_bf16)
    k2 = [k1[h] - count(x1_s[h], t1[h], True) for h in range(halves)]
    x0_s = [[jnp.where(x1_s[h][c] == t1[h], b0_s[h][c], minus1) for c in range(nslab)] for h in range(halves)]
    t0 = search(x0_s, k2, 8, int_to_bf16)
    rows = []
    for h in range(halves):
        need = k2[h] - count(x0_s[h], t0[h], True)
        off = jnp.zeros((rh, LANES), F32)
        parts = []
        for c in range(nslab):
            above = (hi_s[h][c] > thr[h]) | (x1_s[h][c] > t1[h]) | (x0_s[h][c] > t0[h])
            eq = jnp.where(x0_s[h][c] == t0[h], one, zero)
            rank_ok = jnp.where(_dot(eq, ut) + off <= need, 1.0, 0.0).astype(BF16)
            parts.append(jnp.where(above, one, eq * rank_ok))
            off = off + _dot(eq, ones)
        rows.append(jnp.concatenate(parts, axis=1))
    return jnp.concatenate(rows, axis=0)


def _prompt_group_calls(kernel_fn, name, q_inputs, seq_inputs, const_inputs, b, s_len, qb):
    nq = s_len // qb
    outs = []
    for g in range(nq):
        n_keys = (g + 1) * qb
        in_specs = [pl.BlockSpec((None, qb, a.shape[2]), lambda bi, g=g: (bi, g, 0)) for a in q_inputs]
        in_specs += [pl.BlockSpec((None, n_keys, a.shape[2]), lambda bi: (bi, 0, 0)) for a in seq_inputs]
        in_specs += [_const_spec(a.shape) for a in const_inputs]
        args = list(q_inputs) + list(seq_inputs) + list(const_inputs)
        outs.append(pl.pallas_call(
            functools.partial(kernel_fn, g=g, qb=qb, n_keys=n_keys),
            grid=(b,),
            in_specs=in_specs,
            out_specs=pl.BlockSpec((None, qb, 1024), lambda bi: (bi, 0, 0)),
            out_shape=jax.ShapeDtypeStruct((b, qb, 1024), BF16),
            compiler_params=pltpu.CompilerParams(dimension_semantics=("parallel",), vmem_limit_bytes=VMEM_LIMIT),
            name=f"{name}_g{g}",
        )(*args))
    return outs


def _dsa_prompt_kernel(qi_ref, iw_ref, q_ref, kik_ref, kb_ref, vb_ref, ut_ref, o_ref, *, topk, g, qb, n_keys):
    qpos = g * qb + lax.broadcasted_iota(I32, (qb, 1), 0)
    kpos = lax.broadcasted_iota(I32, (1, n_keys), 1)
    valid = kpos <= qpos
    if n_keys <= topk:
        mask = valid
    else:
        iw = iw_ref[...]
        score = jnp.zeros((qb, n_keys), F32)
        for j in range(N_IDX_HEADS // 2):
            qs = qi_ref[:, j * LANES:(j + 1) * LANES]
            l0 = _dot_nt(qs, kik_ref[:, :LANES])
            l1 = _dot_nt(qs, kik_ref[:, LANES:])
            score = (score + jnp.maximum(l0, 0.0) * iw[:, 2 * j:2 * j + 1]
                     + jnp.maximum(l1, 0.0) * iw[:, 2 * j + 1:2 * j + 2])
        score = jnp.where(valid, score, NEG_FILL)
        mask = _topk_select(score, topk, ut_ref[...]).astype(F32) > 0.5
        if g * qb < topk:
            mask = mask & valid
    rep = N_HEADS_A // N_KV_A
    for kv in range(N_KV_A):
        kh = kb_ref[:, kv * LANES:(kv + 1) * LANES]
        vh = vb_ref[:, 2 * kv * LANES:(2 * kv + 2) * LANES]
        for r in range(rep):
            hh = kv * rep + r
            s = jnp.where(mask, _dot_nt(q_ref[:, hh * LANES:(hh + 1) * LANES], kh), NEG_INF)
            m = jnp.max(s, axis=-1, keepdims=True)
            p = jnp.exp(s - m).astype(BF16)
            pv = _dot(p, vh)
            o_ref[:, hh * LANES:(hh + 1) * LANES] = (pv[:, :LANES] / pv[:, LANES:]).astype(o_ref.dtype)


def _dsa_prompt(qi, iw, kik, q, kb, vb, ut, b, s_len, topk, qb):
    def v3(a):
        return a.reshape(b, s_len, a.shape[-1])

    return _prompt_group_calls(functools.partial(_dsa_prompt_kernel, topk=topk), "dsa_prompt",
                               [v3(qi), v3(iw), v3(q)], [v3(kik), v3(kb), v3(vb)], [ut], b, s_len, qb)


def _mla_prompt_kernel(qn_ref, qpe_ref, kn_ref, kpek_ref, vm_ref, o_ref, *, g, qb, n_keys):
    qpos = g * qb + lax.broadcasted_iota(I32, (qb, 1), 0)
    kpos = (n_keys - qb) + lax.broadcasted_iota(I32, (1, qb), 1)
    diag_ok = kpos <= qpos
    for h in range(N_HEADS_M):
        sl = slice(h * LANES, (h + 1) * LANES)
        qcat = jnp.concatenate([qn_ref[:, sl], qpe_ref[:, (h // 2) * LANES:(h // 2 + 1) * LANES]], axis=1)
        kcat = jnp.concatenate([kn_ref[:, sl], kpek_ref[:, (h % 2) * LANES:(h % 2 + 1) * LANES]], axis=1)
        s = _dot_nt(qcat, kcat)
        s_diag = jnp.where(diag_ok, s[:, n_keys - qb:], NEG_INF)
        s = s_diag if n_keys == qb else jnp.concatenate([s[:, :n_keys - qb], s_diag], axis=1)
        m = jnp.max(s, axis=-1, keepdims=True)
        p = jnp.exp(s - m).astype(BF16)
        r = _dot(p, vm_ref[:, 2 * h * LANES:(2 * h + 2) * LANES])
        o_ref[:, sl] = (r[:, :LANES] / r[:, LANES:]).astype(o_ref.dtype)


def _mla_prompt(qn, qpe, kn, kpek, vm, b, s_len, qb):
    def v3(a):
        return a.reshape(b, s_len, a.shape[-1])

    return _prompt_group_calls(_mla_prompt_kernel, "mla_prompt", [v3(qn), v3(qpe)], [v3(kn), v3(kpek), v3(vm)], [],
                               b, s_len, qb)


def _merge_mlp_kernel(*refs, n_grp, ff_chunk):
    x_ref = refs[0]
    oa_refs = refs[1:1 + n_grp]
    ob_refs = refs[1 + n_grp:1 + 2 * n_grp]
    g_ref, wo_ref, g2_ref, wup_ref, wdn_ref, y_ref, m_ref = refs[1 + 2 * n_grp:]
    d = x_ref.shape[1]
    grp = pl.program_id(0) % n_grp

    def merge(j):
        ga = g_ref[:, :d].astype(F32)
        gb = g_ref[:, d:].astype(F32)
        m_ref[...] = (ga * oa_refs[j][...].astype(F32) + gb * ob_refs[j][...].astype(F32)).astype(BF16)

    for j in range(n_grp):
        pl.when(grp == j)(functools.partial(merge, j))
    x1 = x_ref[...] + _dot(m_ref[...], wo_ref[...])
    ms = jnp.mean(x1 * x1, axis=-1, keepdims=True)
    h2 = (x1 * lax.rsqrt(ms + NORM_EPS) * g2_ref[...]).astype(BF16)
    acc = x1
    for c in range(wup_ref.shape[1] // ff_chunk):
        u = jnp.maximum(_dot(h2, wup_ref[:, c * ff_chunk:(c + 1) * ff_chunk]), 0.0)
        acc = acc + _dot((u * u).astype(BF16), wdn_ref[c * ff_chunk:(c + 1) * ff_chunk, :])
    y_ref[...] = acc


def _merge_mlp(x2d, oa_list, ob_list, g, wo, g2, wup, wdn, tm):
    n, d = x2d.shape
    n_grp = len(oa_list)
    assert len(ob_list) == n_grp and all(a.shape[1:] == (tm, d) for a in oa_list + ob_list)

    def row(w):
        return pl.BlockSpec((tm, w), lambda i: (i, 0))

    grp_spec = pl.BlockSpec((None, tm, d), lambda i: (i // n_grp, 0, 0))
    return pl.pallas_call(
        functools.partial(_merge_mlp_kernel, n_grp=n_grp, ff_chunk=1024),
        grid=(n // tm,),
        in_specs=[row(d)] + [grp_spec] * (2 * n_grp) + [row(2 * d), _const_spec(wo.shape), _const_spec(g2.shape),
                                                        _const_spec(wup.shape), _const_spec(wdn.shape)],
        out_specs=row(d),
        out_shape=jax.ShapeDtypeStruct((n, d), F32),
        scratch_shapes=[pltpu.VMEM((tm, d), BF16)],
        compiler_params=pltpu.CompilerParams(dimension_semantics=("arbitrary",), vmem_limit_bytes=VMEM_LIMIT),
        name="merge_mlp",
    )(x2d, *oa_list, *ob_list, g, wo, g2, wup, wdn)


def _page_copies_t(pt_ref, cache_ref, buf_ref, sem_ref, seq, slot, n_pages, page):
    return [pltpu.make_async_copy(cache_ref.at[pt_ref[seq * n_pages + p]],
                                  buf_ref.at[slot, :, pl.ds(p * page, page)],
                                  sem_ref.at[slot]) for p in range(n_pages)]


def _samp_score_kernel(pt_ref, qi_ref, wc_ref, kin_ref, cache_ref, o_ref, buf_ref, sem_ref, *, n_pages, page, chunk):
    s = pl.program_id(0)
    ns = pl.num_programs(0)
    slot = s % 2
    past = n_pages * page

    @pl.when(s == 0)
    def _():
        for cp in _page_copies_t(pt_ref, cache_ref, buf_ref, sem_ref, 0, 0, n_pages, page):
            cp.start()

    @pl.when(s + 1 < ns)
    def _():
        for cp in _page_copies_t(pt_ref, cache_ref, buf_ref, sem_ref, s + 1, 1 - slot, n_pages, page):
            cp.start()

    pltpu.make_async_copy(buf_ref.at[1 - slot], buf_ref.at[slot], sem_ref.at[slot]).wait()

    qi = qi_ref[0]
    wc = wc_ref[0]

    def score_of(keys_t):
        l = _dot(qi, keys_t.astype(BF16))
        return jnp.sum(jnp.maximum(l, 0.0) * wc, axis=0, keepdims=True)

    for c in range(past // chunk):
        o_ref[0, :, c * chunk:(c + 1) * chunk] = score_of(buf_ref[slot, :, c * chunk:(c + 1) * chunk])
    lane = lax.broadcasted_iota(I32, (1, LANES), 1)
    tail_keys = jnp.where(lane == 0, kin_ref[0], 0.0)
    o_ref[0, :, past:] = jnp.where(lane == 0, score_of(tail_keys), NEG_FILL)


def _samp_scores(pt_flat, qi3, wc3, kin3, cache_idx_t, n_pages, page):
    ns = qi3.shape[0]
    past = n_pages * page
    chunk = min(2048, past)
    grid_spec = pltpu.PrefetchScalarGridSpec(
        num_scalar_prefetch=1,
        grid=(ns,),
        in_specs=[pl.BlockSpec((1, N_IDX_HEADS, IDX_DIM), lambda s, pt: (s, 0, 0)),
                  pl.BlockSpec((1, N_IDX_HEADS, 1), lambda s, pt: (s, 0, 0)),
                  pl.BlockSpec((1, IDX_DIM, 1), lambda s, pt: (s, 0, 0)),
                  pl.BlockSpec(memory_space=pl.ANY)],
        out_specs=pl.BlockSpec((1, 1, past + LANES), lambda s, pt: (s, 0, 0)),
        scratch_shapes=[pltpu.VMEM((2, IDX_DIM, past), F32), pltpu.SemaphoreType.DMA((2,))],
    )
    return pl.pallas_call(
        functools.partial(_samp_score_kernel, n_pages=n_pages, page=page, chunk=chunk),
        grid_spec=grid_spec,
        out_shape=jax.ShapeDtypeStruct((ns, 1, past + LANES), F32),
        compiler_params=pltpu.CompilerParams(dimension_semantics=("arbitrary",), vmem_limit_bytes=VMEM_LIMIT),
        name="sample_scores",
    )(pt_flat, qi3, wc3, kin3, cache_idx_t)


def _samp_select_kernel(score_ref, ut_ref, j8_ref, idx_ref, np_ref, rank_ref, *, topk, past, chunk):
    ns = score_ref.shape[0]
    sel = _topk_select(score_ref[...], topk, ut_ref[...])
    off = jnp.zeros((ns, 1), F32)
    for c in range(past // LANES):
        sl = slice(c * LANES, (c + 1) * LANES)
        sc = sel[:, sl]
        pc = _dot(sc, ut_ref[...])
        rank_ref[:, sl] = jnp.where(sc.astype(F32) > 0.5, pc + off, 0.0)
        off = off + pc[:, LANES - 1:LANES]
    np_ref[...] = jnp.broadcast_to(off, np_ref.shape).astype(I32)
    want = (lax.broadcasted_iota(I32, (topk, 1), 0) + 1).astype(F32)

    def per_seq(s, carry):
        acc = jnp.zeros((8, topk), F32)
        for c in range(past // chunk):
            rr = rank_ref[pl.ds(s, 1), c * chunk:(c + 1) * chunk]
            onehot = jnp.where(rr == want, 1.0, 0.0).astype(BF16)
            acc = acc + _dot_nt(j8_ref[:, c * chunk:(c + 1) * chunk], onehot)
        idx_ref[pl.ds(s, 1), :] = (acc[0:1, :] * float(LANES) + acc[1:2, :]).astype(I32)
        return carry

    lax.fori_loop(0, ns, per_seq, 0, unroll=2)


def _samp_select(score2d, ut, j8, topk, past):
    ns, n = score2d.shape
    chunk = min(1024, past)
    return pl.pallas_call(
        functools.partial(_samp_select_kernel, topk=topk, past=past, chunk=chunk),
        grid=(1,),
        in_specs=[pl.BlockSpec((ns, n), lambda i: (0, 0)), pl.BlockSpec(ut.shape, lambda i: (0, 0)),
                  pl.BlockSpec(j8.shape, lambda i: (0, 0))],
        out_specs=[pl.BlockSpec((ns, topk), lambda i: (0, 0)), pl.BlockSpec((ns, LANES), lambda i: (0, 0))],
        out_shape=[jax.ShapeDtypeStruct((ns, topk), I32), jax.ShapeDtypeStruct((ns, LANES), I32)],
        scratch_shapes=[pltpu.VMEM((ns, past), F32)],
        compiler_params=pltpu.CompilerParams(dimension_semantics=("arbitrary",), vmem_limit_bytes=VMEM_LIMIT),
        name="sample_select",
    )(score2d, ut, j8)


def _dsa_rows_attend(q_ref, kn_ref, vn_ref, kbuf, vbuf, o_ref, n_past, topk):
    slot_ok = lax.broadcasted_iota(I32, (1, topk), 1) < n_past
    new_ok = n_past < topk
    rep = N_HEADS_A // N_KV_A
    for kv in range(N_KV_A):
        kh = kbuf[:, kv, :].astype(BF16)
        vh = vbuf[:, kv, :].astype(BF16)
        q2 = q_ref[0, kv * rep:(kv + 1) * rep, :]
        k_new = kn_ref[0, kv:kv + 1, :].astype(BF16).astype(F32)
        v_new = vn_ref[0, kv:kv + 1, :].astype(BF16).astype(F32)
        sc = jnp.where(slot_ok, _dot_nt(q2, kh), NEG_INF)
        s_new = jnp.sum(q2.astype(F32) * k_new, axis=-1, keepdims=True)
        s_new = jnp.where(new_ok, s_new, NEG_INF)
        m = jnp.maximum(jnp.max(sc, axis=-1, keepdims=True), s_new)
        p = jnp.exp(sc - m)
        p_new = jnp.exp(s_new - m)
        l = jnp.sum(p, axis=-1, keepdims=True) + p_new
        o = (_dot(p.astype(BF16), vh) + p_new.astype(BF16).astype(F32) * v_new) / l
        for g in range(rep):
            hh = kv * rep + g
            o_ref[0, :, hh * LANES:(hh + 1) * LANES] = o[g:g + 1, :]


def _page_copies(pt_ref, cache_ref, buf_ref, sem_ref, seq, slot, n_pages, page):
    return [pltpu.make_async_copy(cache_ref.at[pt_ref[seq * n_pages + p]],
                                  buf_ref.at[slot, pl.ds(p * page, page)],
                                  sem_ref.at[slot]) for p in range(n_pages)]


def _samp_attn_kernel(pt_ref, idx_ref, np_ref, qn_ref, qcol_ref, gcol_ref, qpe_ref, cn_ref, knn_ref, kpn_ref,
                      wukt_ref, wuktf_ref, wuv_ref, qa_ref, kna_ref, vna_ref, cc_ref, cp_ref, ck_ref, cv_ref,
                      o_ref, oa_ref, cbuf, pbuf, s_ref, kbuf, vbuf, sem_ref, row_sem, *, n_pages, page, chunk, topk):
    s = pl.program_id(0)
    ns = pl.num_programs(0)
    slot = s % 2
    past = n_pages * page
    n_chunks = past // chunk
    rows_per_chunk = topk // n_chunks

    def copies(seq, sl):
        return (_page_copies(pt_ref, cc_ref, cbuf, sem_ref.at[0], seq, sl, n_pages, page)
                + _page_copies_t(pt_ref, cp_ref, pbuf, sem_ref.at[1], seq, sl, n_pages, page))

    @pl.when(s == 0)
    def _():
        for cp in copies(0, 0):
            cp.start()

    @pl.when(s + 1 < ns)
    def _():
        for cp in copies(s + 1, 1 - slot):
            cp.start()

    pltpu.make_async_copy(cbuf.at[1 - slot], cbuf.at[slot], sem_ref.at[0, slot]).wait()
    pltpu.make_async_copy(pbuf.at[1 - slot], pbuf.at[slot], sem_ref.at[1, slot]).wait()

    qg = qcol_ref[0].astype(F32) * gcol_ref[...]
    prod = wuktf_ref[...] * qg
    q_abs = jnp.concatenate([jnp.sum(prod[h * LANES:(h + 1) * LANES, :], axis=0, keepdims=True)
                             for h in range(N_HEADS_M)], axis=0).astype(BF16)
    qpe = qpe_ref[0]

    def score_chunk(c, carry):
        for r in range(rows_per_chunk):
            row = c * rows_per_chunk + r
            pos = idx_ref[s * topk + row]
            pg = pt_ref[s * n_pages + pos // page]
            off = pos % page
            pltpu.make_async_copy(ck_ref.at[pg, off], kbuf.at[row], row_sem.at[0]).start()
            pltpu.make_async_copy(cv_ref.at[pg, off], vbuf.at[row], row_sem.at[1]).start()
        start = pl.multiple_of(c * chunk, chunk)
        cb = cbuf[slot, pl.ds(start, chunk), :].astype(BF16)
        knt = _dot_nt(wukt_ref[...], cb)
        ssq = jnp.concatenate([jnp.sum(jnp.square(knt[h * LANES:(h + 1) * LANES, :]), axis=0, keepdims=True)
                               for h in range(N_HEADS_M)], axis=0)
        s_nope = _dot_nt(q_abs, cb) * lax.rsqrt(ssq * (1.0 / NOPE_DIM) + NORM_EPS)
        s_pe = _dot(qpe, pbuf[slot, :, pl.ds(start, chunk)].astype(BF16))
        s_ref[:, pl.ds(start, chunk)] = s_nope + s_pe
        return carry

    lax.fori_loop(0, n_chunks, score_chunk, 0)
    qn = qn_ref[0].astype(F32)
    s_new = (jnp.sum(qn * knn_ref[0].astype(F32), axis=-1, keepdims=True)
             + jnp.sum(qpe.astype(F32) * kpn_ref[0].astype(BF16).astype(F32), axis=-1, keepdims=True))
    lane = lax.broadcasted_iota(I32, (1, LANES), 1)
    s_ref[:, past:] = jnp.where(lane == 0, s_new, NEG_INF)

    sc = s_ref[...]
    m = jnp.max(sc, axis=-1, keepdims=True)
    p = jnp.exp(sc - m)
    l = jnp.sum(p, axis=-1, keepdims=True)
    s_ref[...] = p
    p_new = jnp.exp(s_new - m)

    def pv_chunk(c, acc):
        start = pl.multiple_of(c * chunk, chunk)
        cb = cbuf[slot, pl.ds(start, chunk), :].astype(BF16)
        return acc + _dot(s_ref[:, pl.ds(start, chunk)].astype(BF16), cb)

    o_lat = lax.fori_loop(0, n_chunks, pv_chunk, jnp.zeros((N_HEADS_M, KV_LORA), F32))
    o_lat = (o_lat + p_new.astype(BF16).astype(F32) * cn_ref[0].astype(BF16).astype(F32)) / l
    o8 = _dot(o_lat.astype(BF16), wuv_ref[...])
    for h in range(N_HEADS_M):
        o_ref[0, :, h * LANES:(h + 1) * LANES] = o8[h:h + 1, h * LANES:(h + 1) * LANES]

    pltpu.make_async_copy(vbuf, kbuf, row_sem.at[0]).wait()
    pltpu.make_async_copy(kbuf, vbuf, row_sem.at[1]).wait()
    _dsa_rows_attend(qa_ref, kna_ref, vna_ref, kbuf, vbuf, oa_ref, np_ref[s], topk)


def _samp_attn(pt_flat, idx_flat, n_past, qn3, qcol, gcol, qpe3, cn3, knn3, kpn3, wukt, wukt_f32, wuv,
               qa3, kna3, vna3, cache_ckv, cache_kpe_t, cache_k, cache_v, n_pages, page, topk):
    ns = qn3.shape[0]
    past = n_pages * page
    chunk = min(1024, past)
    assert topk % (past // chunk) == 0

    def per_seq(shape):
        return pl.BlockSpec((1,) + shape, lambda s, *_: (s, 0, 0))

    def const(shape):
        nd = len(shape)
        return pl.BlockSpec(shape, lambda s, *_: (0,) * nd, pipeline_mode=pl.Buffered(1))

    hbm = pl.BlockSpec(memory_space=pl.ANY)
    grid_spec = pltpu.PrefetchScalarGridSpec(
        num_scalar_prefetch=3,
        grid=(ns,),
        in_specs=[per_seq((N_HEADS_M, NOPE_DIM)), per_seq((N_HEADS_M * NOPE_DIM, 1)), const(gcol.shape),
                  per_seq((N_HEADS_M, ROPE_DIM_M)), per_seq((1, KV_LORA)), per_seq((N_HEADS_M, NOPE_DIM)),
                  per_seq((1, ROPE_DIM_M)), const(wukt.shape), const(wukt_f32.shape), const(wuv.shape),
                  per_seq((N_HEADS_A, HEAD_DIM_A)), per_seq((N_KV_A, HEAD_DIM_A)), per_seq((N_KV_A, HEAD_DIM_A)),
                  hbm, hbm, hbm, hbm],
        out_specs=[per_seq((1, N_HEADS_M * V_DIM_M)), per_seq((1, N_HEADS_A * HEAD_DIM_A))],
        scratch_shapes=[pltpu.VMEM((2, past, KV_LORA), F32), pltpu.VMEM((2, ROPE_DIM_M, past), F32),
                        pltpu.VMEM((N_HEADS_M, past + LANES), F32),
                        pltpu.VMEM((topk, N_KV_A, HEAD_DIM_A), F32), pltpu.VMEM((topk, N_KV_A, HEAD_DIM_A), F32),
                        pltpu.SemaphoreType.DMA((2, 2)), pltpu.SemaphoreType.DMA((2,))],
    )
    out_row = jax.ShapeDtypeStruct((ns, 1, N_HEADS_M * V_DIM_M), F32)
    return pl.pallas_call(
        functools.partial(_samp_attn_kernel, n_pages=n_pages, page=page, chunk=chunk, topk=topk),
        grid_spec=grid_spec,
        out_shape=[out_row, out_row],
        compiler_params=pltpu.CompilerParams(dimension_semantics=("arbitrary",), vmem_limit_bytes=VMEM_LIMIT),
        name="sample_attn",
    )(pt_flat, idx_flat, n_past, qn3, qcol, gcol, qpe3, cn3, knn3, kpn3, wukt, wukt_f32, wuv,
      qa3, kna3, vna3, cache_ckv, cache_kpe_t, cache_k, cache_v)


def _prep_weights(norm1_g, w_in, a_qn_g, a_kn_g, q_a_norm_g, w_q_b, kv_a_norm_g, w_uk, w_uv,
                  m_qn_g, m_kn_g, m_qr_g, m_kr_g):
    d = w_in.shape[0]
    sizes = (1024, 512, 512, 512, IDX_DIM, N_IDX_HEADS, Q_LORA, KV_LORA, ROPE_DIM_M, 2 * d)
    offs = [0]
    for sz in sizes:
        offs.append(offs[-1] + sz)
    wq, wk, wv, wqi, wki, wiw, wqa, wkva, wpe, wg = [w_in[:, offs[i]:offs[i + 1]] for i in range(len(sizes))]
    w_perm = jnp.concatenate(
        [wq, wk, wv, wqi, wki, wpe, wiw, jnp.zeros((d, LANES - N_IDX_HEADS), w_in.dtype), wqa, wkva, wg],
        axis=1).astype(BF16)
    wqb = jnp.concatenate([w_q_b[:, :, :NOPE_DIM].reshape(Q_LORA, -1), w_q_b[:, :, NOPE_DIM:].reshape(Q_LORA, -1)],
                          axis=1).astype(BF16)

    def rowv(v):
        return v.astype(F32).reshape(1, -1)

    return {
        "g1": rowv(norm1_g), "w_in": w_perm, "gq": rowv(a_qn_g), "gk": rowv(a_kn_g), "gqa": rowv(q_a_norm_g),
        "wqb": wqb, "gkva": rowv(kv_a_norm_g), "wuk": w_uk.reshape(KV_LORA, -1).astype(BF16),
        "wuv": w_uv.reshape(KV_LORA, -1).astype(BF16),
        "gqn": rowv(m_qn_g), "gkn": rowv(m_kn_g), "gqr": rowv(jnp.tile(m_qr_g, 2)),
        "gkp": rowv(jnp.concatenate([jnp.ones((IDX_DIM,), F32), m_kr_g.astype(F32)])),
    }


def kernel(x_prompt, x_sample, cache_k, cache_v, cache_idx_k, cache_ckv, cache_kpe, page_table,
           norm1_g, w_in, a_qn_g, a_kn_g, q_a_norm_g, w_q_b, kv_a_norm_g, w_uk, w_uv,
           m_qn_g, m_kn_g, m_qr_g, m_kr_g, w_o, norm2_g, w_up, w_down):
    b, s_len, d = x_prompt.shape
    ns, t_s, _ = x_sample.shape
    page = cache_k.shape[1]
    n_pages = page_table.shape[1]
    past = n_pages * page
    assert d == N_HEADS_A * HEAD_DIM_A and t_s == 1
    assert w_in.shape[1] == C_END - (LANES - N_IDX_HEADS)

    wts = _prep_weights(norm1_g, w_in, a_qn_g, a_kn_g, q_a_norm_g, w_q_b, kv_a_norm_g, w_uk, w_uv,
                        m_qn_g, m_kn_g, m_qr_g, m_kr_g)
    wo = w_o.astype(BF16)
    wup = w_up.astype(BF16)
    wdn = w_down.astype(BF16)
    g2 = norm2_g.astype(F32).reshape(1, -1)
    ut = (jnp.arange(LANES)[:, None] <= jnp.arange(LANES)[None, :]).astype(BF16)

    tm = min(256, s_len)
    qb = min(PROMPT_Q_BLOCK, s_len)
    tab_p = _rope_tables(jnp.arange(s_len))
    xp = x_prompt.reshape(b * s_len, d)
    (q, kf, kb, vf, vb, qi, kif, kpef, kik, kpek, iw, qn, qpe, cf, kn, vm, g) = _project(xp, tab_p, s_len // tm, tm, wts)
    topk_p = min(TOPK_MAX, s_len // 4)
    oa = _dsa_prompt(qi, iw, kik, q, kb, vb, ut, b, s_len, topk_p, qb)
    ob = _mla_prompt(qn, qpe, kn, kpek, vm, b, s_len, qb)
    y_prompt = _merge_mlp(xp, oa, ob, g, wo, g2, wup, wdn, qb).reshape(b, s_len, d)

    tab_s = _rope_tables(jnp.full((ns,), past, I32))
    xs = x_sample.reshape(ns, d)
    (sq, skf, _skb, svf, _svb, sqi, skif, skpef, _skik, _skpek, siw, sqn, sqpe, scf, skn, _svm, sg) = _project(
        xs, tab_s, 1, ns, wts)
    topk_s = min(TOPK_MAX, (past + 1) // 4)
    pt_flat = page_table.reshape(-1).astype(I32)
    cache_idx_t = jnp.swapaxes(cache_idx_k, 1, 2)
    cache_kpe_t = jnp.swapaxes(cache_kpe, 1, 2)
    score = _samp_scores(pt_flat, sqi.reshape(ns, N_IDX_HEADS, IDX_DIM), siw[:, :N_IDX_HEADS].reshape(ns, N_IDX_HEADS, 1),
                         skif.reshape(ns, IDX_DIM, 1), cache_idx_t, n_pages, page)
    pos = jnp.arange(past)
    j8 = jnp.zeros((8, past), F32).at[0].set(pos // LANES).at[1].set(pos % LANES).astype(BF16)
    idx, n_past = _samp_select(score.reshape(ns, past + LANES), ut, j8, topk_s, past)
    gcol = jnp.tile(m_kn_g.astype(F32), N_HEADS_M).reshape(-1, 1)
    wukt_f32 = w_uk.reshape(KV_LORA, -1).T.astype(F32)
    ob_s, oa_s = _samp_attn(pt_flat, idx.reshape(-1), n_past[:, 0],
                            sqn.reshape(ns, N_HEADS_M, NOPE_DIM), sqn.reshape(ns, N_HEADS_M * NOPE_DIM, 1), gcol,
                            sqpe.reshape(ns, N_HEADS_M, ROPE_DIM_M), scf.reshape(ns, 1, KV_LORA),
                            skn.reshape(ns, N_HEADS_M, NOPE_DIM), skpef.reshape(ns, 1, ROPE_DIM_M),
                            wukt_f32.astype(BF16), wukt_f32, wts["wuv"],
                            sq.reshape(ns, N_HEADS_A, HEAD_DIM_A), skf, svf,
                            cache_ckv, cache_kpe_t, cache_k, cache_v, n_pages, page, topk_s)
    y_sample = _merge_mlp(xs, [oa_s.reshape(1, ns, d)], [ob_s.reshape(1, ns, d)], sg, wo, g2, wup, wdn,
                          ns).reshape(ns, 1, d)

    return (y_prompt, y_sample,
            kf.reshape(b, s_len, N_KV_A, HEAD_DIM_A), vf.reshape(b, s_len, N_KV_A, HEAD_DIM_A),
            kif.reshape(b, s_len, IDX_DIM), cf.reshape(b, s_len, KV_LORA), kpef.reshape(b, s_len, ROPE_DIM_M),
            skf.reshape(ns, 1, N_KV_A, HEAD_DIM_A), svf.reshape(ns, 1, N_KV_A, HEAD_DIM_A),
            skif.reshape(ns, 1, IDX_DIM), scf.reshape(ns, 1, KV_LORA), skpef.reshape(ns, 1, ROPE_DIM_M))
```

```python
import functools

import jax
import jax.numpy as jnp
from jax import lax
from jax.experimental import pallas as pl
from jax.experimental.pallas import tpu as pltpu

F32 = jnp.float32
BF16 = jnp.bfloat16
I32 = jnp.int32

N_HEADS_A = 8
HEAD_DIM_A = 128
N_KV_A = 4
ROT_A = 32
N_IDX_HEADS = 8
IDX_DIM = 64
ROT_IDX = 16
TOPK_MAX = 256
N_HEADS_M = 8
Q_LORA = 256
KV_LORA = 256
NOPE_DIM = 128
ROPE_DIM_M = 64
V_DIM_M = 128
LOG2_E = 1.4426950408889634
DSA_SCALE = HEAD_DIM_A ** -0.5 * LOG2_E
MLA_SCALE = (NOPE_DIM + ROPE_DIM_M) ** -0.5 * LOG2_E
ROPE_THETA = 500000.0
NORM_EPS = 1e-6
LANES = 128
NEG_INF = float("-inf")
NEG_FILL = -3.0e38

C_Q, C_K, C_V, C_QI, C_KP, C_IW, C_QA, C_KVA, C_G, C_END = 0, 1024, 1536, 2048, 2560, 2688, 2816, 3072, 3328, 5376

T_A, T_I, T_M, T_KP, N_TAB = 0, 3, 6, 9, 14

VMEM_LIMIT = 56 * 1024 * 1024
PROMPT_Q_BLOCK = 256
SEARCH_UNROLL = 4


def _dot(a, b):
    return jnp.dot(a, b, preferred_element_type=F32)


def _dot_nt(a, b):
    return lax.dot_general(a, b, (((1,), (1,)), ((), ())), preferred_element_type=F32)


def _rms(z, g):
    ms = jnp.mean(z * z, axis=-1, keepdims=True)
    return z * lax.rsqrt(ms + NORM_EPS) * g


def _rope3(y, c, s1, s2, shift):
    return y * c + pltpu.roll(y, shift, 1) * s1 + pltpu.roll(y, LANES - shift, 1) * s2


def _const_spec(shape):
    nd = len(shape)
    return pl.BlockSpec(shape, lambda *_: (0,) * nd, pipeline_mode=pl.Buffered(1))


def _rope_tables(pos):
    pos = pos.astype(F32)
    lane = jnp.arange(LANES)

    def tables(rot, period, lanes_sel=None):
        half = rot // 2
        inv = jnp.power(jnp.float32(ROPE_THETA), -jnp.arange(half, dtype=F32) * (2.0 / rot))
        ang = pos[:, None] * inv[None, :]
        cos, sin = jnp.cos(ang), jnp.sin(ang)
        lp = lane % period
        in_rot = lp < rot
        first = lp < half
        second = in_rot & ~first
        cidx = lp % half
        c = jnp.where(in_rot[None, :], cos[:, cidx], 1.0)
        s1 = jnp.where(second[None, :], sin[:, cidx], 0.0)
        s2 = jnp.where(first[None, :], -sin[:, cidx], 0.0)
        if lanes_sel is not None:
            c = jnp.where(lanes_sel[None, :], c, 1.0)
            s1 = jnp.where(lanes_sel[None, :], s1, 0.0)
            s2 = jnp.where(lanes_sel[None, :], s2, 0.0)
        return c, s1, s2

    ca, s1a, s2a = tables(ROT_A, HEAD_DIM_A)
    ci, s1i, s2i = tables(ROT_IDX, IDX_DIM)
    cm, s1m, s2m = tables(ROPE_DIM_M, ROPE_DIM_M)
    lo = lane < IDX_DIM
    cil, s1il, s2il = tables(ROT_IDX, IDX_DIM, lo)
    cmh, s1mh, s2mh = tables(ROPE_DIM_M, ROPE_DIM_M, ~lo)
    ckp = jnp.where(lo[None, :], cil, cmh)
    return jnp.concatenate([ca, s1a, s2a, ci, s1i, s2i, cm, s1m, s2m, ckp, s1il, s2il, s1mh, s2mh], axis=1)


def _proj_kernel(x_ref, tab_ref, g1_ref, w_ref, gq_ref, gk_ref, gqa_ref, wqb_ref, gkva_ref, wuk_ref, wuv_ref,
                 gqn_ref, gkn_ref, gqr_ref, gkp_ref,
                 q_o, kf_o, kb_o, vf_o, vb_o, qi_o, kif_o, kpef_o, kik_o, kpek_o, iw_o,
                 qn_o, qpe_o, cf_o, kn_o, vm_o, g_o):
    def tab(i):
        return tab_ref[:, i * LANES:(i + 1) * LANES]

    x = x_ref[...]
    ms = jnp.mean(x * x, axis=-1, keepdims=True)
    h = (x * lax.rsqrt(ms + NORM_EPS) * g1_ref[...]).astype(BF16)

    lane = lax.broadcasted_iota(I32, (1, LANES), 1)
    lo = lane < IDX_DIM

    def seg_rms64(z, g):
        sq = z * z
        s_lo = jnp.sum(jnp.where(lo, sq, 0.0), axis=-1, keepdims=True)
        s_hi = jnp.sum(jnp.where(lo, 0.0, sq), axis=-1, keepdims=True)
        ms64 = jnp.where(lo, s_lo, s_hi) * (1.0 / ROPE_DIM_M)
        return z * lax.rsqrt(ms64 + NORM_EPS) * g

    z = _dot(h, w_ref[:, C_Q:C_K])
    for c in range(N_HEADS_A):
        y = _rms(z[:, c * LANES:(c + 1) * LANES], gq_ref[...])
        y = _rope3(y, tab(T_A), tab(T_A + 1), tab(T_A + 2), ROT_A // 2)
        q_o[:, c * LANES:(c + 1) * LANES] = (y * DSA_SCALE).astype(BF16)
    z = _dot(h, w_ref[:, C_K:C_V])
    for c in range(N_KV_A):
        y = _rms(z[:, c * LANES:(c + 1) * LANES], gk_ref[...])
        y = _rope3(y, tab(T_A), tab(T_A + 1), tab(T_A + 2), ROT_A // 2)
        kf_o[:, c, :] = y
        kb_o[:, c * LANES:(c + 1) * LANES] = y.astype(BF16)
    z = _dot(h, w_ref[:, C_V:C_QI])
    for c in range(N_KV_A):
        vf_o[:, c, :] = z[:, c * LANES:(c + 1) * LANES]
    ones_blk = jnp.ones((z.shape[0], LANES), BF16)
    for c in range(N_KV_A):
        vb_o[:, 2 * c * LANES:(2 * c + 1) * LANES] = z[:, c * LANES:(c + 1) * LANES].astype(BF16)
        vb_o[:, (2 * c + 1) * LANES:(2 * c + 2) * LANES] = ones_blk
    z = _dot(h, w_ref[:, C_QI:C_KP])
    for c in range(N_IDX_HEADS * IDX_DIM // LANES):
        y = _rope3(z[:, c * LANES:(c + 1) * LANES], tab(T_I), tab(T_I + 1), tab(T_I + 2), ROT_IDX // 2)
        qi_o[:, c * LANES:(c + 1) * LANES] = y.astype(BF16)
    z = _dot(h, w_ref[:, C_KP:C_IW])
    sq = z * z
    s_hi = jnp.sum(jnp.where(lo, 0.0, sq), axis=-1, keepdims=True)
    r_hi = lax.rsqrt(s_hi * (1.0 / ROPE_DIM_M) + NORM_EPS)
    y = jnp.where(lo, z, z * r_hi * gkp_ref[...])
    y = (y * tab(T_KP)
         + pltpu.roll(y, ROT_IDX // 2, 1) * tab(T_KP + 1)
         + pltpu.roll(y, LANES - ROT_IDX // 2, 1) * tab(T_KP + 2)
         + pltpu.roll(y, ROPE_DIM_M // 2, 1) * tab(T_KP + 3)
         + pltpu.roll(y, LANES - ROPE_DIM_M // 2, 1) * tab(T_KP + 4))
    kif_o[...] = y[:, :IDX_DIM]
    kpef_o[...] = y[:, IDX_DIM:]
    ki_lo = jnp.where(lo, y, 0.0)
    kpe_hi = jnp.where(lo, 0.0, y)
    kik_o[:, :LANES] = ki_lo.astype(BF16)
    kik_o[:, LANES:] = pltpu.roll(ki_lo, IDX_DIM, 1).astype(BF16)
    kpek_o[:, :LANES] = pltpu.roll(kpe_hi, IDX_DIM, 1).astype(BF16)
    kpek_o[:, LANES:] = kpe_hi.astype(BF16)
    z = _dot(h, w_ref[:, C_IW:C_QA])
    iw_o[...] = z * (N_IDX_HEADS ** -0.5) * (IDX_DIM ** -0.5)
    z = _dot(h, w_ref[:, C_QA:C_KVA])
    qa = _rms(z, gqa_ref[...]).astype(BF16)
    qm = _dot(qa, wqb_ref[...])
    for c in range(N_HEADS_M):
        y = _rms(qm[:, c * LANES:(c + 1) * LANES], gqn_ref[...])
        qn_o[:, c * LANES:(c + 1) * LANES] = (y * MLA_SCALE).astype(BF16)
    base = N_HEADS_M * NOPE_DIM
    for c in range(N_HEADS_M * ROPE_DIM_M // LANES):
        y = seg_rms64(qm[:, base + c * LANES:base + (c + 1) * LANES], gqr_ref[...])
        y = _rope3(y, tab(T_M), tab(T_M + 1), tab(T_M + 2), ROPE_DIM_M // 2)
        qpe_o[:, c * LANES:(c + 1) * LANES] = (y * MLA_SCALE).astype(BF16)
    z = _dot(h, w_ref[:, C_KVA:C_G])
    ckv = _rms(z, gkva_ref[...])
    cf_o[...] = ckv
    cb = ckv.astype(BF16)
    knr = _dot(cb, wuk_ref[...])
    for c in range(N_HEADS_M):
        y = _rms(knr[:, c * LANES:(c + 1) * LANES], gkn_ref[...])
        kn_o[:, c * LANES:(c + 1) * LANES] = y.astype(BF16)
    vmz = _dot(cb, wuv_ref[...])
    for c in range(N_HEADS_M):
        vm_o[:, 2 * c * LANES:(2 * c + 1) * LANES] = vmz[:, c * LANES:(c + 1) * LANES].astype(BF16)
        vm_o[:, (2 * c + 1) * LANES:(2 * c + 2) * LANES] = ones_blk
    z = _dot(h, w_ref[:, C_G:C_END])
    g_o[...] = (1.0 / (1.0 + jnp.exp(-z))).astype(g_o.dtype)


def _project(x2d, tab, n_pos_tiles, tm, wts):
    n, d = x2d.shape
    grid = (n // tm,)

    def row(width):
        return pl.BlockSpec((tm, width), lambda i: (i, 0))

    kv3 = pl.BlockSpec((tm, N_KV_A, HEAD_DIM_A), lambda i: (i, 0, 0))
    kv3_shape = jax.ShapeDtypeStruct((n, N_KV_A, HEAD_DIM_A), F32)
    out_defs = [
        (1024, BF16), None, (512, BF16), None, (1024, BF16), (512, BF16),
        (IDX_DIM, F32), (ROPE_DIM_M, F32), (256, BF16), (256, BF16), (LANES, F32),
        (1024, BF16), (512, BF16), (KV_LORA, F32), (1024, BF16), (2048, BF16), (2048, BF16),
    ]
    consts = [wts["g1"], wts["w_in"], wts["gq"], wts["gk"], wts["gqa"], wts["wqb"], wts["gkva"], wts["wuk"],
              wts["wuv"], wts["gqn"], wts["gkn"], wts["gqr"], wts["gkp"]]
    in_specs = [row(d), pl.BlockSpec((tm, N_TAB * LANES), lambda i: (i % n_pos_tiles, 0))]
    in_specs += [_const_spec(c.shape) for c in consts]
    return pl.pallas_call(
        _proj_kernel,
        grid=grid,
        in_specs=in_specs,
        out_specs=[kv3 if od is None else row(od[0]) for od in out_defs],
        out_shape=[kv3_shape if od is None else jax.ShapeDtypeStruct((n, od[0]), od[1]) for od in out_defs],
        compiler_params=pltpu.CompilerParams(dimension_semantics=("parallel",), vmem_limit_bytes=VMEM_LIMIT),
        name="project",
    )(x2d, tab, *consts)


def _topk_select(score, k, ut):
    r, n = score.shape
    nslab = n // LANES
    halves = 2 if r % 32 == 0 else 1
    rh = r // halves
    one = jnp.asarray(1, BF16)
    zero = jnp.asarray(0, BF16)
    minus1 = jnp.asarray(-1, BF16)
    ones = jnp.ones((LANES, LANES), BF16)

    bits = jnp.where(score == 0.0, 0, lax.bitcast_convert_type(score, I32))
    hi = lax.bitcast_convert_type(bits & jnp.int32(-65536), F32).astype(BF16)
    low = bits & 0xFFFF
    low = jnp.where(bits < 0, 0xFFFF - low, low)
    b1 = (low >> 8).astype(F32).astype(BF16)
    b0 = (low & 0xFF).astype(F32).astype(BF16)

    def slabs(x):
        return [[x[h * rh:(h + 1) * rh, c * LANES:(c + 1) * LANES] for c in range(nslab)] for h in range(halves)]

    def count(xs, cand, strict):
        acc = jnp.zeros((rh, LANES), BF16)
        for x in xs:
            acc = acc + jnp.where((x > cand) if strict else (x >= cand), one, zero)
        return _dot(acc, ones)

    def int_to_bf16(u):
        return u.astype(F32).astype(BF16)

    def pattern_to_bf16(u):
        raw = jnp.where(u >= 0x8000, u ^ 0x8000, u ^ 0xFFFF)
        return lax.bitcast_convert_type(lax.shift_left(raw, 16), F32).astype(BF16)

    def search(xs, kk, n_bits, to_value):
        def body(b, ts):
            bit = lax.shift_left(jnp.int32(1), n_bits - 1 - b)
            out = []
            for h in range(halves):
                cand = ts[h] | bit
                out.append(jnp.where(count(xs[h], to_value(cand), False) >= kk[h], cand, ts[h]))
            return tuple(out)

        ts = lax.fori_loop(0, n_bits, body, tuple(jnp.zeros((rh, LANES), I32) for _ in range(halves)),
                           unroll=SEARCH_UNROLL)
        return [to_value(t) for t in ts]

    hi_s, b1_s, b0_s = slabs(hi), slabs(b1), slabs(b0)
    k0 = [jnp.full((rh, LANES), k, F32) for _ in range(halves)]
    thr = search(hi_s, k0, 16, pattern_to_bf16)
    k1 = [k0[h] - count(hi_s[h], thr[h], True) for h in range(halves)]
    x1_s = [[jnp.where(hi_s[h][c] == thr[h], b1_s[h][c], minus1) for c in range(nslab)] for h in range(halves)]
    t1 = search(x1_s, k1, 8, int_to_bf16)
    k2 = [k1[h] - count(x1_s[h], t1[h], True) for h in range(halves)]
    x0_s = [[jnp.where(x1_s[h][c] == t1[h], b0_s[h][c], minus1) for c in range(nslab)] for h in range(halves)]
    t0 = search(x0_s, k2, 8, int_to_bf16)
    rows = []
    for h in range(halves):
        need = k2[h] - count(x0_s[h], t0[h], True)
        off = jnp.zeros((rh, LANES), F32)
        parts = []
        for c in range(nslab):
            above = (hi_s[h][c] > thr[h]) | (x1_s[h][c] > t1[h]) | (x0_s[h][c] > t0[h])
            eq = jnp.where(x0_s[h][c] == t0[h], one, zero)
            rank_ok = jnp.where(_dot(eq, ut) + off <= need, 1.0, 0.0).astype(BF16)
            parts.append(jnp.where(above, one, eq * rank_ok))
            off = off + _dot(eq, ones)
        rows.append(jnp.concatenate(parts, axis=1))
    return jnp.concatenate(rows, axis=0)


def _prompt_group_calls(kernel_fn, name, q_inputs, seq_inputs, const_inputs, b, s_len, qb):
    nq = s_len // qb
    outs = []
    for g in range(nq):
        n_keys = (g + 1) * qb
        in_specs = [pl.BlockSpec((None, qb, a.shape[2]), lambda bi, g=g: (bi, g, 0)) for a in q_inputs]
        in_specs += [pl.BlockSpec((None, n_keys, a.shape[2]), lambda bi: (bi, 0, 0)) for a in seq_inputs]
        in_specs += [_const_spec(a.shape) for a in const_inputs]
        args = list(q_inputs) + list(seq_inputs) + list(const_inputs)
        outs.append(pl.pallas_call(
            functools.partial(kernel_fn, g=g, qb=qb, n_keys=n_keys),
            grid=(b,),
            in_specs=in_specs,
            out_specs=pl.BlockSpec((None, qb, 1024), lambda bi: (bi, 0, 0)),
            out_shape=jax.ShapeDtypeStruct((b, qb, 1024), BF16),
            compiler_params=pltpu.CompilerParams(dimension_semantics=("parallel",), vmem_limit_bytes=VMEM_LIMIT),
            name=f"{name}_g{g}",
        )(*args))
    return outs


def _dsa_prompt_kernel(qi_ref, iw_ref, q_ref, kik_ref, kb_ref, vb_ref, ut_ref, o_ref, *, topk, g, qb, n_keys):
    qpos = g * qb + lax.broadcasted_iota(I32, (qb, 1), 0)
    kpos = lax.broadcasted_iota(I32, (1, n_keys), 1)
    valid = kpos <= qpos
    if n_keys <= topk:
        mask = valid
    else:
        iw = iw_ref[...]
        score = jnp.zeros((qb, n_keys), F32)
        for j in range(N_IDX_HEADS // 2):
            qs = qi_ref[:, j * LANES:(j + 1) * LANES]
            l0 = _dot_nt(qs, kik_ref[:, :LANES])
            l1 = _dot_nt(qs, kik_ref[:, LANES:])
            score = (score + jnp.maximum(l0, 0.0) * iw[:, 2 * j:2 * j + 1]
                     + jnp.maximum(l1, 0.0) * iw[:, 2 * j + 1:2 * j + 2])
        score = jnp.where(valid, score, NEG_FILL)
        mask = _topk_select(score, topk, ut_ref[...]).astype(F32) > 0.5
        if g * qb < topk:
            mask = mask & valid
    bias = jnp.where(mask, 0.0, NEG_INF)
    rep = N_HEADS_A // N_KV_A
    for kv in range(N_KV_A):
        kh = kb_ref[:, kv * LANES:(kv + 1) * LANES]
        vh = vb_ref[:, 2 * kv * LANES:(2 * kv + 2) * LANES]
        for r in range(rep):
            hh = kv * rep + r
            s = _dot_nt(q_ref[:, hh * LANES:(hh + 1) * LANES], kh) + bias
            m = jnp.max(s, axis=-1, keepdims=True)
            p = jnp.exp2(s - m).astype(BF16)
            pv = _dot(p, vh)
            o_ref[:, hh * LANES:(hh + 1) * LANES] = (pv[:, :LANES] / pv[:, LANES:]).astype(o_ref.dtype)


def _dsa_prompt(qi, iw, kik, q, kb, vb, ut, b, s_len, topk, qb):
    def v3(a):
        return a.reshape(b, s_len, a.shape[-1])

    return _prompt_group_calls(functools.partial(_dsa_prompt_kernel, topk=topk), "dsa_prompt",
                               [v3(qi), v3(iw), v3(q)], [v3(kik), v3(kb), v3(vb)], [ut], b, s_len, qb)


def _mla_prompt_kernel(qn_ref, qpe_ref, kn_ref, kpek_ref, vm_ref, o_ref, *, g, qb, n_keys):
    qpos = g * qb + lax.broadcasted_iota(I32, (qb, 1), 0)
    kpos = (n_keys - qb) + lax.broadcasted_iota(I32, (1, qb), 1)
    diag_ok = kpos <= qpos
    for h in range(N_HEADS_M):
        sl = slice(h * LANES, (h + 1) * LANES)
        qcat = jnp.concatenate([qn_ref[:, sl], qpe_ref[:, (h // 2) * LANES:(h // 2 + 1) * LANES]], axis=1)
        kcat = jnp.concatenate([kn_ref[:, sl], kpek_ref[:, (h % 2) * LANES:(h % 2 + 1) * LANES]], axis=1)
        s = _dot_nt(qcat, kcat)
        s_diag = jnp.where(diag_ok, s[:, n_keys - qb:], NEG_INF)
        s = s_diag if n_keys == qb else jnp.concatenate([s[:, :n_keys - qb], s_diag], axis=1)
        m = jnp.max(s, axis=-1, keepdims=True)
        p = jnp.exp2(s - m).astype(BF16)
        r = _dot(p, vm_ref[:, 2 * h * LANES:(2 * h + 2) * LANES])
        o_ref[:, sl] = (r[:, :LANES] / r[:, LANES:]).astype(o_ref.dtype)


def _mla_prompt(qn, qpe, kn, kpek, vm, b, s_len, qb):
    def v3(a):
        return a.reshape(b, s_len, a.shape[-1])

    return _prompt_group_calls(_mla_prompt_kernel, "mla_prompt", [v3(qn), v3(qpe)], [v3(kn), v3(kpek), v3(vm)], [],
                               b, s_len, qb)


def _merge_mlp_kernel(*refs, n_grp, ff_chunk):
    x_ref = refs[0]
    oa_refs = refs[1:1 + n_grp]
    ob_refs = refs[1 + n_grp:1 + 2 * n_grp]
    g_ref, wo_ref, g2_ref, wup_ref, wdn_ref, y_ref, m_ref = refs[1 + 2 * n_grp:]
    d = x_ref.shape[1]
    grp = pl.program_id(0) % n_grp

    def merge(j):
        ga = g_ref[:, :d].astype(F32)
        gb = g_ref[:, d:].astype(F32)
        m_ref[...] = (ga * oa_refs[j][...].astype(F32) + gb * ob_refs[j][...].astype(F32)).astype(BF16)

    for j in range(n_grp):
        pl.when(grp == j)(functools.partial(merge, j))
    x1 = x_ref[...] + _dot(m_ref[...], wo_ref[...])
    ms = jnp.mean(x1 * x1, axis=-1, keepdims=True)
    h2 = (x1 * lax.rsqrt(ms + NORM_EPS) * g2_ref[...]).astype(BF16)
    acc = x1
    for c in range(wup_ref.shape[1] // ff_chunk):
        u = jnp.maximum(_dot(h2, wup_ref[:, c * ff_chunk:(c + 1) * ff_chunk]), 0.0)
        acc = acc + _dot((u * u).astype(BF16), wdn_ref[c * ff_chunk:(c + 1) * ff_chunk, :])
    y_ref[...] = acc


def _merge_mlp(x2d, oa_list, ob_list, g, wo, g2, wup, wdn, tm):
    n, d = x2d.shape
    n_grp = len(oa_list)
    assert len(ob_list) == n_grp and all(a.shape[1:] == (tm, d) for a in oa_list + ob_list)

    def row(w):
        return pl.BlockSpec((tm, w), lambda i: (i, 0))

    grp_spec = pl.BlockSpec((None, tm, d), lambda i: (i // n_grp, 0, 0))
    return pl.pallas_call(
        functools.partial(_merge_mlp_kernel, n_grp=n_grp, ff_chunk=1024),
        grid=(n // tm,),
        in_specs=[row(d)] + [grp_spec] * (2 * n_grp) + [row(2 * d), _const_spec(wo.shape), _const_spec(g2.shape),
                                                        _const_spec(wup.shape), _const_spec(wdn.shape)],
        out_specs=row(d),
        out_shape=jax.ShapeDtypeStruct((n, d), F32),
        scratch_shapes=[pltpu.VMEM((tm, d), BF16)],
        compiler_params=pltpu.CompilerParams(dimension_semantics=("arbitrary",), vmem_limit_bytes=VMEM_LIMIT),
        name="merge_mlp",
    )(x2d, *oa_list, *ob_list, g, wo, g2, wup, wdn)


def _page_copies_t(pt_ref, cache_ref, buf_ref, sem_ref, seq, slot, n_pages, page):
    return [pltpu.make_async_copy(cache_ref.at[pt_ref[seq * n_pages + p]],
                                  buf_ref.at[slot, :, pl.ds(p * page, page)],
                                  sem_ref.at[slot]) for p in range(n_pages)]


def _samp_score_kernel(pt_ref, qi_ref, wc_ref, kin_ref, cache_ref, o_ref, buf_ref, sem_ref, *, n_pages, page, chunk):
    s = pl.program_id(0)
    ns = pl.num_programs(0)
    slot = s % 2
    past = n_pages * page

    @pl.when(s == 0)
    def _():
        for cp in _page_copies_t(pt_ref, cache_ref, buf_ref, sem_ref, 0, 0, n_pages, page):
            cp.start()

    @pl.when(s + 1 < ns)
    def _():
        for cp in _page_copies_t(pt_ref, cache_ref, buf_ref, sem_ref, s + 1, 1 - slot, n_pages, page):
            cp.start()

    pltpu.make_async_copy(buf_ref.at[1 - slot], buf_ref.at[slot], sem_ref.at[slot]).wait()

    qi = qi_ref[0]
    wc = wc_ref[0]

    def score_of(keys_t):
        l = _dot(qi, keys_t.astype(BF16))
        return jnp.sum(jnp.maximum(l, 0.0) * wc, axis=0, keepdims=True)

    for c in range(past // chunk):
        o_ref[0, :, c * chunk:(c + 1) * chunk] = score_of(buf_ref[slot, :, c * chunk:(c + 1) * chunk])
    lane = lax.broadcasted_iota(I32, (1, LANES), 1)
    tail_keys = jnp.where(lane == 0, kin_ref[0], 0.0)
    o_ref[0, :, past:] = jnp.where(lane == 0, score_of(tail_keys), NEG_FILL)


def _samp_scores(pt_flat, qi3, wc3, kin3, cache_idx_t, n_pages, page):
    ns = qi3.shape[0]
    past = n_pages * page
    chunk = min(2048, past)
    grid_spec = pltpu.PrefetchScalarGridSpec(
        num_scalar_prefetch=1,
        grid=(ns,),
        in_specs=[pl.BlockSpec((1, N_IDX_HEADS, IDX_DIM), lambda s, pt: (s, 0, 0)),
                  pl.BlockSpec((1, N_IDX_HEADS, 1), lambda s, pt: (s, 0, 0)),
                  pl.BlockSpec((1, IDX_DIM, 1), lambda s, pt: (s, 0, 0)),
                  pl.BlockSpec(memory_space=pl.ANY)],
        out_specs=pl.BlockSpec((1, 1, past + LANES), lambda s, pt: (s, 0, 0)),
        scratch_shapes=[pltpu.VMEM((2, IDX_DIM, past), F32), pltpu.SemaphoreType.DMA((2,))],
    )
    return pl.pallas_call(
        functools.partial(_samp_score_kernel, n_pages=n_pages, page=page, chunk=chunk),
        grid_spec=grid_spec,
        out_shape=jax.ShapeDtypeStruct((ns, 1, past + LANES), F32),
        compiler_params=pltpu.CompilerParams(dimension_semantics=("arbitrary",), vmem_limit_bytes=VMEM_LIMIT),
        name="sample_scores",
    )(pt_flat, qi3, wc3, kin3, cache_idx_t)


def _samp_select_kernel(score_ref, ut_ref, j8_ref, idx_ref, np_ref, rank_ref, *, topk, past, chunk):
    ns = score_ref.shape[0]
    sel = _topk_select(score_ref[...], topk, ut_ref[...])
    off = jnp.zeros((ns, 1), F32)
    for c in range(past // LANES):
        sl = slice(c * LANES, (c + 1) * LANES)
        sc = sel[:, sl]
        pc = _dot(sc, ut_ref[...])
        rank_ref[:, sl] = jnp.where(sc.astype(F32) > 0.5, pc + off, 0.0)
        off = off + pc[:, LANES - 1:LANES]
    np_ref[...] = jnp.broadcast_to(off, np_ref.shape).astype(I32)
    want = (lax.broadcasted_iota(I32, (topk, 1), 0) + 1).astype(F32)

    def per_seq(s, carry):
        acc = jnp.zeros((8, topk), F32)
        for c in range(past // chunk):
            rr = rank_ref[pl.ds(s, 1), c * chunk:(c + 1) * chunk]
            onehot = jnp.where(rr == want, 1.0, 0.0).astype(BF16)
            acc = acc + _dot_nt(j8_ref[:, c * chunk:(c + 1) * chunk], onehot)
        idx_ref[pl.ds(s, 1), :] = (acc[0:1, :] * float(LANES) + acc[1:2, :]).astype(I32)
        return carry

    lax.fori_loop(0, ns, per_seq, 0, unroll=2)


def _samp_select(score2d, ut, j8, topk, past):
    ns, n = score2d.shape
    chunk = min(1024, past)
    return pl.pallas_call(
        functools.partial(_samp_select_kernel, topk=topk, past=past, chunk=chunk),
        grid=(1,),
        in_specs=[pl.BlockSpec((ns, n), lambda i: (0, 0)), pl.BlockSpec(ut.shape, lambda i: (0, 0)),
                  pl.BlockSpec(j8.shape, lambda i: (0, 0))],
        out_specs=[pl.BlockSpec((ns, topk), lambda i: (0, 0)), pl.BlockSpec((ns, LANES), lambda i: (0, 0))],
        out_shape=[jax.ShapeDtypeStruct((ns, topk), I32), jax.ShapeDtypeStruct((ns, LANES), I32)],
        scratch_shapes=[pltpu.VMEM((ns, past), F32)],
        compiler_params=pltpu.CompilerParams(dimension_semantics=("arbitrary",), vmem_limit_bytes=VMEM_LIMIT),
        name="sample_select",
    )(score2d, ut, j8)


def _dsa_rows_attend(q_ref, kn_ref, vn_ref, kbuf, vbuf, o_ref, n_past, topk):
    slot_ok = lax.broadcasted_iota(I32, (1, topk), 1) < n_past
    new_ok = n_past < topk
    rep = N_HEADS_A // N_KV_A
    for kv in range(N_KV_A):
        kh = kbuf[:, kv, :].astype(BF16)
        vh = vbuf[:, kv, :].astype(BF16)
        q2 = q_ref[0, kv * rep:(kv + 1) * rep, :]
        k_new = kn_ref[0, kv:kv + 1, :].astype(BF16).astype(F32)
        v_new = vn_ref[0, kv:kv + 1, :].astype(BF16).astype(F32)
        sc = jnp.where(slot_ok, _dot_nt(q2, kh), NEG_INF)
        s_new = jnp.sum(q2.astype(F32) * k_new, axis=-1, keepdims=True)
        s_new = jnp.where(new_ok, s_new, NEG_INF)
        m = jnp.maximum(jnp.max(sc, axis=-1, keepdims=True), s_new)
        p = jnp.exp2(sc - m)
        p_new = jnp.exp2(s_new - m)
        l = jnp.sum(p, axis=-1, keepdims=True) + p_new
        o = (_dot(p.astype(BF16), vh) + p_new.astype(BF16).astype(F32) * v_new) / l
        for g in range(rep):
            hh = kv * rep + g
            o_ref[0, :, hh * LANES:(hh + 1) * LANES] = o[g:g + 1, :]


def _page_copies(pt_ref, cache_ref, buf_ref, sem_ref, seq, slot, n_pages, page):
    return [pltpu.make_async_copy(cache_ref.at[pt_ref[seq * n_pages + p]],
                                  buf_ref.at[slot, pl.ds(p * page, page)],
                                  sem_ref.at[slot]) for p in range(n_pages)]


def _samp_attn_kernel(pt_ref, idx_ref, np_ref, qn_ref, qcol_ref, gcol_ref, qpe_ref, cn_ref, knn_ref, kpn_ref,
                      wukt_ref, wuktf_ref, wuv_ref, qa_ref, kna_ref, vna_ref, cc_ref, cp_ref, ck_ref, cv_ref,
                      o_ref, oa_ref, cbuf, pbuf, s_ref, kbuf, vbuf, sem_ref, row_sem, *, n_pages, page, chunk, topk):
    s = pl.program_id(0)
    ns = pl.num_programs(0)
    slot = s % 2
    past = n_pages * page
    n_chunks = past // chunk
    rows_per_chunk = topk // n_chunks

    def copies(seq, sl):
        return (_page_copies(pt_ref, cc_ref, cbuf, sem_ref.at[0], seq, sl, n_pages, page)
                + _page_copies_t(pt_ref, cp_ref, pbuf, sem_ref.at[1], seq, sl, n_pages, page))

    @pl.when(s == 0)
    def _():
        for cp in copies(0, 0):
            cp.start()

    @pl.when(s + 1 < ns)
    def _():
        for cp in copies(s + 1, 1 - slot):
            cp.start()

    pltpu.make_async_copy(cbuf.at[1 - slot], cbuf.at[slot], sem_ref.at[0, slot]).wait()
    pltpu.make_async_copy(pbuf.at[1 - slot], pbuf.at[slot], sem_ref.at[1, slot]).wait()

    qg = qcol_ref[0].astype(F32) * gcol_ref[...]
    prod = wuktf_ref[...] * qg
    q_abs = jnp.concatenate([jnp.sum(prod[h * LANES:(h + 1) * LANES, :], axis=0, keepdims=True)
                             for h in range(N_HEADS_M)], axis=0).astype(BF16)
    qpe = qpe_ref[0]

    def score_chunk(c, carry):
        for r in range(rows_per_chunk):
            row = c * rows_per_chunk + r
            pos = idx_ref[s * topk + row]
            pg = pt_ref[s * n_pages + pos // page]
            off = pos % page
            pltpu.make_async_copy(ck_ref.at[pg, off], kbuf.at[row], row_sem.at[0]).start()
            pltpu.make_async_copy(cv_ref.at[pg, off], vbuf.at[row], row_sem.at[1]).start()
        start = pl.multiple_of(c * chunk, chunk)
        cb = cbuf[slot, pl.ds(start, chunk), :].astype(BF16)
        knt = _dot_nt(wukt_ref[...], cb)
        ssq = jnp.concatenate([jnp.sum(jnp.square(knt[h * LANES:(h + 1) * LANES, :]), axis=0, keepdims=True)
                               for h in range(N_HEADS_M)], axis=0)
        s_nope = _dot_nt(q_abs, cb) * lax.rsqrt(ssq * (1.0 / NOPE_DIM) + NORM_EPS)
        s_pe = _dot(qpe, pbuf[slot, :, pl.ds(start, chunk)].astype(BF16))
        s_ref[:, pl.ds(start, chunk)] = s_nope + s_pe
        return carry

    lax.fori_loop(0, n_chunks, score_chunk, 0)
    qn = qn_ref[0].astype(F32)
    s_new = (jnp.sum(qn * knn_ref[0].astype(F32), axis=-1, keepdims=True)
             + jnp.sum(qpe.astype(F32) * kpn_ref[0].astype(BF16).astype(F32), axis=-1, keepdims=True))
    lane = lax.broadcasted_iota(I32, (1, LANES), 1)
    s_ref[:, past:] = jnp.where(lane == 0, s_new, NEG_INF)

    sc = s_ref[...]
    m = jnp.max(sc, axis=-1, keepdims=True)
    p = jnp.exp2(sc - m)
    l = jnp.sum(p, axis=-1, keepdims=True)
    s_ref[...] = p
    p_new = jnp.exp2(s_new - m)

    def pv_chunk(c, acc):
        start = pl.multiple_of(c * chunk, chunk)
        cb = cbuf[slot, pl.ds(start, chunk), :].astype(BF16)
        return acc + _dot(s_ref[:, pl.ds(start, chunk)].astype(BF16), cb)

    o_lat = lax.fori_loop(0, n_chunks, pv_chunk, jnp.zeros((N_HEADS_M, KV_LORA), F32))
    o_lat = (o_lat + p_new.astype(BF16).astype(F32) * cn_ref[0].astype(BF16).astype(F32)) / l
    o8 = _dot(o_lat.astype(BF16), wuv_ref[...])
    for h in range(N_HEADS_M):
        o_ref[0, :, h * LANES:(h + 1) * LANES] = o8[h:h + 1, h * LANES:(h + 1) * LANES]

    pltpu.make_async_copy(vbuf, kbuf, row_sem.at[0]).wait()
    pltpu.make_async_copy(kbuf, vbuf, row_sem.at[1]).wait()
    _dsa_rows_attend(qa_ref, kna_ref, vna_ref, kbuf, vbuf, oa_ref, np_ref[s], topk)


def _samp_attn(pt_flat, idx_flat, n_past, qn3, qcol, gcol, qpe3, cn3, knn3, kpn3, wukt, wukt_f32, wuv,
               qa3, kna3, vna3, cache_ckv, cache_kpe_t, cache_k, cache_v, n_pages, page, topk):
    ns = qn3.shape[0]
    past = n_pages * page
    chunk = min(1024, past)
    assert topk % (past // chunk) == 0

    def per_seq(shape):
        return pl.BlockSpec((1,) + shape, lambda s, *_: (s, 0, 0))

    def const(shape):
        nd = len(shape)
        return pl.BlockSpec(shape, lambda s, *_: (0,) * nd, pipeline_mode=pl.Buffered(1))

    hbm = pl.BlockSpec(memory_space=pl.ANY)
    grid_spec = pltpu.PrefetchScalarGridSpec(
        num_scalar_prefetch=3,
        grid=(ns,),
        in_specs=[per_seq((N_HEADS_M, NOPE_DIM)), per_seq((N_HEADS_M * NOPE_DIM, 1)), const(gcol.shape),
                  per_seq((N_HEADS_M, ROPE_DIM_M)), per_seq((1, KV_LORA)), per_seq((N_HEADS_M, NOPE_DIM)),
                  per_seq((1, ROPE_DIM_M)), const(wukt.shape), const(wukt_f32.shape), const(wuv.shape),
                  per_seq((N_HEADS_A, HEAD_DIM_A)), per_seq((N_KV_A, HEAD_DIM_A)), per_seq((N_KV_A, HEAD_DIM_A)),
                  hbm, hbm, hbm, hbm],
        out_specs=[per_seq((1, N_HEADS_M * V_DIM_M)), per_seq((1, N_HEADS_A * HEAD_DIM_A))],
        scratch_shapes=[pltpu.VMEM((2, past, KV_LORA), F32), pltpu.VMEM((2, ROPE_DIM_M, past), F32),
                        pltpu.VMEM((N_HEADS_M, past + LANES), F32),
                        pltpu.VMEM((topk, N_KV_A, HEAD_DIM_A), F32), pltpu.VMEM((topk, N_KV_A, HEAD_DIM_A), F32),
                        pltpu.SemaphoreType.DMA((2, 2)), pltpu.SemaphoreType.DMA((2,))],
    )
    out_row = jax.ShapeDtypeStruct((ns, 1, N_HEADS_M * V_DIM_M), F32)
    return pl.pallas_call(
        functools.partial(_samp_attn_kernel, n_pages=n_pages, page=page, chunk=chunk, topk=topk),
        grid_spec=grid_spec,
        out_shape=[out_row, out_row],
        compiler_params=pltpu.CompilerParams(dimension_semantics=("arbitrary",), vmem_limit_bytes=VMEM_LIMIT),
        name="sample_attn",
    )(pt_flat, idx_flat, n_past, qn3, qcol, gcol, qpe3, cn3, knn3, kpn3, wukt, wukt_f32, wuv,
      qa3, kna3, vna3, cache_ckv, cache_kpe_t, cache_k, cache_v)


def _prep_weights(norm1_g, w_in, a_qn_g, a_kn_g, q_a_norm_g, w_q_b, kv_a_norm_g, w_uk, w_uv,
                  m_qn_g, m_kn_g, m_qr_g, m_kr_g):
    d = w_in.shape[0]
    sizes = (1024, 512, 512, 512, IDX_DIM, N_IDX_HEADS, Q_LORA, KV_LORA, ROPE_DIM_M, 2 * d)
    offs = [0]
    for sz in sizes:
        offs.append(offs[-1] + sz)
    wq, wk, wv, wqi, wki, wiw, wqa, wkva, wpe, wg = [w_in[:, offs[i]:offs[i + 1]] for i in range(len(sizes))]
    w_perm = jnp.concatenate(
        [wq, wk, wv, wqi, wki, wpe, wiw, jnp.zeros((d, LANES - N_IDX_HEADS), w_in.dtype), wqa, wkva, wg],
        axis=1).astype(BF16)
    wqb = jnp.concatenate([w_q_b[:, :, :NOPE_DIM].reshape(Q_LORA, -1), w_q_b[:, :, NOPE_DIM:].reshape(Q_LORA, -1)],
                          axis=1).astype(BF16)

    def rowv(v):
        return v.astype(F32).reshape(1, -1)

    return {
        "g1": rowv(norm1_g), "w_in": w_perm, "gq": rowv(a_qn_g), "gk": rowv(a_kn_g), "gqa": rowv(q_a_norm_g),
        "wqb": wqb, "gkva": rowv(kv_a_norm_g), "wuk": w_uk.reshape(KV_LORA, -1).astype(BF16),
        "wuv": w_uv.reshape(KV_LORA, -1).astype(BF16),
        "gqn": rowv(m_qn_g), "gkn": rowv(m_kn_g), "gqr": rowv(jnp.tile(m_qr_g, 2)),
        "gkp": rowv(jnp.concatenate([jnp.ones((IDX_DIM,), F32), m_kr_g.astype(F32)])),
    }


def kernel(x_prompt, x_sample, cache_k, cache_v, cache_idx_k, cache_ckv, cache_kpe, page_table,
           norm1_g, w_in, a_qn_g, a_kn_g, q_a_norm_g, w_q_b, kv_a_norm_g, w_uk, w_uv,
           m_qn_g, m_kn_g, m_qr_g, m_kr_g, w_o, norm2_g, w_up, w_down):
    b, s_len, d = x_prompt.shape
    ns, t_s, _ = x_sample.shape
    page = cache_k.shape[1]
    n_pages = page_table.shape[1]
    past = n_pages * page
    assert d == N_HEADS_A * HEAD_DIM_A and t_s == 1
    assert w_in.shape[1] == C_END - (LANES - N_IDX_HEADS)

    wts = _prep_weights(norm1_g, w_in, a_qn_g, a_kn_g, q_a_norm_g, w_q_b, kv_a_norm_g, w_uk, w_uv,
                        m_qn_g, m_kn_g, m_qr_g, m_kr_g)
    wo = w_o.astype(BF16)
    wup = w_up.astype(BF16)
    wdn = w_down.astype(BF16)
    g2 = norm2_g.astype(F32).reshape(1, -1)
    ut = (jnp.arange(LANES)[:, None] <= jnp.arange(LANES)[None, :]).astype(BF16)

    tm = min(256, s_len)
    qb = min(PROMPT_Q_BLOCK, s_len)
    tab_p = _rope_tables(jnp.arange(s_len))
    xp = x_prompt.reshape(b * s_len, d)
    (q, kf, kb, vf, vb, qi, kif, kpef, kik, kpek, iw, qn, qpe, cf, kn, vm, g) = _project(xp, tab_p, s_len // tm, tm, wts)
    topk_p = min(TOPK_MAX, s_len // 4)
    oa = _dsa_prompt(qi, iw, kik, q, kb, vb, ut, b, s_len, topk_p, qb)
    ob = _mla_prompt(qn, qpe, kn, kpek, vm, b, s_len, qb)
    y_prompt = _merge_mlp(xp, oa, ob, g, wo, g2, wup, wdn, qb).reshape(b, s_len, d)

    tab_s = _rope_tables(jnp.full((ns,), past, I32))
    xs = x_sample.reshape(ns, d)
    (sq, skf, _skb, svf, _svb, sqi, skif, skpef, _skik, _skpek, siw, sqn, sqpe, scf, skn, _svm, sg) = _project(
        xs, tab_s, 1, ns, wts)
    topk_s = min(TOPK_MAX, (past + 1) // 4)
    pt_flat = page_table.reshape(-1).astype(I32)
    cache_idx_t = jnp.swapaxes(cache_idx_k, 1, 2)
    cache_kpe_t = jnp.swapaxes(cache_kpe, 1, 2)
    score = _samp_scores(pt_flat, sqi.reshape(ns, N_IDX_HEADS, IDX_DIM), siw[:, :N_IDX_HEADS].reshape(ns, N_IDX_HEADS, 1),
                         skif.reshape(ns, IDX_DIM, 1), cache_idx_t, n_pages, page)
    pos = jnp.arange(past)
    j8 = jnp.zeros((8, past), F32).at[0].set(pos // LANES).at[1].set(pos % LANES).astype(BF16)
    idx, n_past = _samp_select(score.reshape(ns, past + LANES), ut, j8, topk_s, past)
    gcol = jnp.tile(m_kn_g.astype(F32), N_HEADS_M).reshape(-1, 1)
    wukt_f32 = w_uk.reshape(KV_LORA, -1).T.astype(F32)
    ob_s, oa_s = _samp_attn(pt_flat, idx.reshape(-1), n_past[:, 0],
                            sqn.reshape(ns, N_HEADS_M, NOPE_DIM), sqn.reshape(ns, N_HEADS_M * NOPE_DIM, 1), gcol,
                            sqpe.reshape(ns, N_HEADS_M, ROPE_DIM_M), scf.reshape(ns, 1, KV_LORA),
                            skn.reshape(ns, N_HEADS_M, NOPE_DIM), skpef.reshape(ns, 1, ROPE_DIM_M),
                            wukt_f32.astype(BF16), wukt_f32, wts["wuv"],
                            sq.reshape(ns, N_HEADS_A, HEAD_DIM_A), skf, svf,
                            cache_ckv, cache_kpe_t, cache_k, cache_v, n_pages, page, topk_s)
    y_sample = _merge_mlp(xs, [oa_s.reshape(1, ns, d)], [ob_s.reshape(1, ns, d)], sg, wo, g2, wup, wdn,
                          ns).reshape(ns, 1, d)

    return (y_prompt, y_sample,
            kf.reshape(b, s_len, N_KV_A, HEAD_DIM_A), vf.reshape(b, s_len, N_KV_A, HEAD_DIM_A),
            kif.reshape(b, s_len, IDX_DIM), cf.reshape(b, s_len, KV_LORA), kpef.reshape(b, s_len, ROPE_DIM_M),
            skf.reshape(ns, 1, N_KV_A, HEAD_DIM_A), svf.reshape(ns, 1, N_KV_A, HEAD_DIM_A),
            skif.reshape(ns, 1, IDX_DIM), scf.reshape(ns, 1, KV_LORA), skpef.reshape(ns, 1, ROPE_DIM_M))
```

```python
import functools

import jax
import jax.numpy as jnp
from jax import lax
from jax.experimental import pallas as pl
from jax.experimental.pallas import tpu as pltpu

F32 = jnp.float32
BF16 = jnp.bfloat16
I32 = jnp.int32

N_HEADS_A = 8
HEAD_DIM_A = 128
N_KV_A = 4
ROT_A = 32
N_IDX_HEADS = 8
IDX_DIM = 64
ROT_IDX = 16
TOPK_MAX = 256
N_HEADS_M = 8
Q_LORA = 256
KV_LORA = 256
NOPE_DIM = 128
ROPE_DIM_M = 64
V_DIM_M = 128
LOG2_E = 1.4426950408889634
DSA_SCALE = HEAD_DIM_A ** -0.5 * LOG2_E
MLA_SCALE = (NOPE_DIM + ROPE_DIM_M) ** -0.5 * LOG2_E
ROPE_THETA = 500000.0
NORM_EPS = 1e-6
LANES = 128
NEG_INF = float("-inf")
NEG_FILL = -3.0e38

C_Q, C_K, C_V, C_QI, C_KP, C_IW, C_QA, C_KVA, C_G, C_END = 0, 1024, 1536, 2048, 2560, 2688, 2816, 3072, 3328, 5376

T_A, T_I, T_M, T_KP, N_TAB = 0, 3, 6, 9, 14

VMEM_LIMIT = 56 * 1024 * 1024
PROMPT_Q_BLOCK = 256
PAGE_DMA_PRIORITY = 1
SEARCH_UNROLL = 4


def _dot(a, b):
    return jnp.dot(a, b, preferred_element_type=F32)


def _dot_nt(a, b):
    return lax.dot_general(a, b, (((1,), (1,)), ((), ())), preferred_element_type=F32)


def _rms(z, g):
    ms = jnp.mean(z * z, axis=-1, keepdims=True)
    return z * lax.rsqrt(ms + NORM_EPS) * g


def _rope3(y, c, s1, s2, shift):
    return y * c + pltpu.roll(y, shift, 1) * s1 + pltpu.roll(y, LANES - shift, 1) * s2


def _const_spec(shape):
    nd = len(shape)
    return pl.BlockSpec(shape, lambda *_: (0,) * nd, pipeline_mode=pl.Buffered(1))


def _rope_tables(pos):
    pos = pos.astype(F32)
    lane = jnp.arange(LANES)

    def tables(rot, period, lanes_sel=None):
        half = rot // 2
        inv = jnp.power(jnp.float32(ROPE_THETA), -jnp.arange(half, dtype=F32) * (2.0 / rot))
        ang = pos[:, None] * inv[None, :]
        cos, sin = jnp.cos(ang), jnp.sin(ang)
        lp = lane % period
        in_rot = lp < rot
        first = lp < half
        second = in_rot & ~first
        cidx = lp % half
        c = jnp.where(in_rot[None, :], cos[:, cidx], 1.0)
        s1 = jnp.where(second[None, :], sin[:, cidx], 0.0)
        s2 = jnp.where(first[None, :], -sin[:, cidx], 0.0)
        if lanes_sel is not None:
            c = jnp.where(lanes_sel[None, :], c, 1.0)
            s1 = jnp.where(lanes_sel[None, :], s1, 0.0)
            s2 = jnp.where(lanes_sel[None, :], s2, 0.0)
        return c, s1, s2

    ca, s1a, s2a = tables(ROT_A, HEAD_DIM_A)
    ci, s1i, s2i = tables(ROT_IDX, IDX_DIM)
    cm, s1m, s2m = tables(ROPE_DIM_M, ROPE_DIM_M)
    lo = lane < IDX_DIM
    cil, s1il, s2il = tables(ROT_IDX, IDX_DIM, lo)
    cmh, s1mh, s2mh = tables(ROPE_DIM_M, ROPE_DIM_M, ~lo)
    ckp = jnp.where(lo[None, :], cil, cmh)
    return jnp.concatenate([ca, s1a, s2a, ci, s1i, s2i, cm, s1m, s2m, ckp, s1il, s2il, s1mh, s2mh], axis=1)


def _proj_kernel(x_ref, tab_ref, g1_ref, w_ref, gq_ref, gk_ref, gqa_ref, wqb_ref, gkva_ref, wuk_ref, wuv_ref,
                 gqn_ref, gkn_ref, gqr_ref, gkp_ref,
                 q_o, kf_o, kb_o, vf_o, vb_o, qi_o, kif_o, kpef_o, kik_o, kpek_o, iw_o,
                 qn_o, qpe_o, cf_o, kn_o, vm_o, g_o):
    def tab(i):
        return tab_ref[:, i * LANES:(i + 1) * LANES]

    x = x_ref[...]
    ms = jnp.mean(x * x, axis=-1, keepdims=True)
    h = (x * lax.rsqrt(ms + NORM_EPS) * g1_ref[...]).astype(BF16)

    lane = lax.broadcasted_iota(I32, (1, LANES), 1)
    lo = lane < IDX_DIM

    def seg_rms64(z, g):
        sq = z * z
        s_lo = jnp.sum(jnp.where(lo, sq, 0.0), axis=-1, keepdims=True)
        s_hi = jnp.sum(jnp.where(lo, 0.0, sq), axis=-1, keepdims=True)
        ms64 = jnp.where(lo, s_lo, s_hi) * (1.0 / ROPE_DIM_M)
        return z * lax.rsqrt(ms64 + NORM_EPS) * g

    z = _dot(h, w_ref[:, C_Q:C_K])
    for c in range(N_HEADS_A):
        y = _rms(z[:, c * LANES:(c + 1) * LANES], gq_ref[...])
        y = _rope3(y, tab(T_A), tab(T_A + 1), tab(T_A + 2), ROT_A // 2)
        q_o[:, c * LANES:(c + 1) * LANES] = (y * DSA_SCALE).astype(BF16)
    z = _dot(h, w_ref[:, C_K:C_V])
    for c in range(N_KV_A):
        y = _rms(z[:, c * LANES:(c + 1) * LANES], gk_ref[...])
        y = _rope3(y, tab(T_A), tab(T_A + 1), tab(T_A + 2), ROT_A // 2)
        kf_o[:, c, :] = y
        kb_o[:, c * LANES:(c + 1) * LANES] = y.astype(BF16)
    z = _dot(h, w_ref[:, C_V:C_QI])
    for c in range(N_KV_A):
        vf_o[:, c, :] = z[:, c * LANES:(c + 1) * LANES]
    ones_blk = jnp.ones((z.shape[0], LANES), BF16)
    for c in range(N_KV_A):
        vb_o[:, 2 * c * LANES:(2 * c + 1) * LANES] = z[:, c * LANES:(c + 1) * LANES].astype(BF16)
        vb_o[:, (2 * c + 1) * LANES:(2 * c + 2) * LANES] = ones_blk
    z = _dot(h, w_ref[:, C_QI:C_KP])
    for c in range(N_IDX_HEADS * IDX_DIM // LANES):
        y = _rope3(z[:, c * LANES:(c + 1) * LANES], tab(T_I), tab(T_I + 1), tab(T_I + 2), ROT_IDX // 2)
        qi_o[:, c * LANES:(c + 1) * LANES] = y.astype(BF16)
    z = _dot(h, w_ref[:, C_KP:C_IW])
    sq = z * z
    s_hi = jnp.sum(jnp.where(lo, 0.0, sq), axis=-1, keepdims=True)
    r_hi = lax.rsqrt(s_hi * (1.0 / ROPE_DIM_M) + NORM_EPS)
    y = jnp.where(lo, z, z * r_hi * gkp_ref[...])
    y = (y * tab(T_KP)
         + pltpu.roll(y, ROT_IDX // 2, 1) * tab(T_KP + 1)
         + pltpu.roll(y, LANES - ROT_IDX // 2, 1) * tab(T_KP + 2)
         + pltpu.roll(y, ROPE_DIM_M // 2, 1) * tab(T_KP + 3)
         + pltpu.roll(y, LANES - ROPE_DIM_M // 2, 1) * tab(T_KP + 4))
    kif_o[...] = y[:, :IDX_DIM]
    kpef_o[...] = y[:, IDX_DIM:]
    ki_lo = jnp.where(lo, y, 0.0)
    kpe_hi = jnp.where(lo, 0.0, y)
    kik_o[:, :LANES] = ki_lo.astype(BF16)
    kik_o[:, LANES:] = pltpu.roll(ki_lo, IDX_DIM, 1).astype(BF16)
    kpek_o[:, :LANES] = pltpu.roll(kpe_hi, IDX_DIM, 1).astype(BF16)
    kpek_o[:, LANES:] = kpe_hi.astype(BF16)
    z = _dot(h, w_ref[:, C_IW:C_QA])
    iw_o[...] = z * (N_IDX_HEADS ** -0.5) * (IDX_DIM ** -0.5)
    z = _dot(h, w_ref[:, C_QA:C_KVA])
    qa = _rms(z, gqa_ref[...]).astype(BF16)
    qm = _dot(qa, wqb_ref[...])
    for c in range(N_HEADS_M):
        y = _rms(qm[:, c * LANES:(c + 1) * LANES], gqn_ref[...])
        qn_o[:, c * LANES:(c + 1) * LANES] = (y * MLA_SCALE).astype(BF16)
    base = N_HEADS_M * NOPE_DIM
    for c in range(N_HEADS_M * ROPE_DIM_M // LANES):
        y = seg_rms64(qm[:, base + c * LANES:base + (c + 1) * LANES], gqr_ref[...])
        y = _rope3(y, tab(T_M), tab(T_M + 1), tab(T_M + 2), ROPE_DIM_M // 2)
        qpe_o[:, c * LANES:(c + 1) * LANES] = (y * MLA_SCALE).astype(BF16)
    z = _dot(h, w_ref[:, C_KVA:C_G])
    ckv = _rms(z, gkva_ref[...])
    cf_o[...] = ckv
    cb = ckv.astype(BF16)
    knr = _dot(cb, wuk_ref[...])
    for c in range(N_HEADS_M):
        y = _rms(knr[:, c * LANES:(c + 1) * LANES], gkn_ref[...])
        kn_o[:, c * LANES:(c + 1) * LANES] = y.astype(BF16)
    vmz = _dot(cb, wuv_ref[...])
    for c in range(N_HEADS_M):
        vm_o[:, 2 * c * LANES:(2 * c + 1) * LANES] = vmz[:, c * LANES:(c + 1) * LANES].astype(BF16)
        vm_o[:, (2 * c + 1) * LANES:(2 * c + 2) * LANES] = ones_blk
    z = _dot(h, w_ref[:, C_G:C_END])
    g_o[...] = (1.0 / (1.0 + jnp.exp(-z))).astype(g_o.dtype)


def _project(x2d, tab, n_pos_tiles, tm, wts):
    n, d = x2d.shape
    grid = (n // tm,)

    def row(width):
        return pl.BlockSpec((tm, width), lambda i: (i, 0))

    kv3 = pl.BlockSpec((tm, N_KV_A, HEAD_DIM_A), lambda i: (i, 0, 0))
    kv3_shape = jax.ShapeDtypeStruct((n, N_KV_A, HEAD_DIM_A), F32)
    out_defs = [
        (1024, BF16), None, (512, BF16), None, (1024, BF16), (512, BF16),
        (IDX_DIM, F32), (ROPE_DIM_M, F32), (256, BF16), (256, BF16), (LANES, F32),
        (1024, BF16), (512, BF16), (KV_LORA, F32), (1024, BF16), (2048, BF16), (2048, BF16),
    ]
    consts = [wts["g1"], wts["w_in"], wts["gq"], wts["gk"], wts["gqa"], wts["wqb"], wts["gkva"], wts["wuk"],
              wts["wuv"], wts["gqn"], wts["gkn"], wts["gqr"], wts["gkp"]]
    in_specs = [row(d), pl.BlockSpec((tm, N_TAB * LANES), lambda i: (i % n_pos_tiles, 0))]
    in_specs += [_const_spec(c.shape) for c in consts]
    return pl.pallas_call(
        _proj_kernel,
        grid=grid,
        in_specs=in_specs,
        out_specs=[kv3 if od is None else row(od[0]) for od in out_defs],
        out_shape=[kv3_shape if od is None else jax.ShapeDtypeStruct((n, od[0]), od[1]) for od in out_defs],
        compiler_params=pltpu.CompilerParams(dimension_semantics=("parallel",), vmem_limit_bytes=VMEM_LIMIT),
        name="project",
    )(x2d, tab, *consts)


def _topk_select(score, k, ut):
    r, n = score.shape
    nslab = n // LANES
    halves = 2 if r % 32 == 0 else 1
    rh = r // halves
    one = jnp.asarray(1, BF16)
    zero = jnp.asarray(0, BF16)
    minus1 = jnp.asarray(-1, BF16)
    ones = jnp.ones((LANES, LANES), BF16)

    bits = jnp.where(score == 0.0, 0, lax.bitcast_convert_type(score, I32))
    hi = lax.bitcast_convert_type(bits & jnp.int32(-65536), F32).astype(BF16)
    low = bits & 0xFFFF
    low = jnp.where(bits < 0, 0xFFFF - low, low)
    b1 = (low >> 8).astype(F32).astype(BF16)
    b0 = (low & 0xFF).astype(F32).astype(BF16)

    def slabs(x):
        return [[x[h * rh:(h + 1) * rh, c * LANES:(c + 1) * LANES] for c in range(nslab)] for h in range(halves)]

    def count(xs, cand, strict):
        acc = jnp.zeros((rh, LANES), BF16)
        for x in xs:
            acc = acc + jnp.where((x > cand) if strict else (x >= cand), one, zero)
        return _dot(acc, ones)

    def int_to_bf16(u):
        return u.astype(F32).astype(BF16)

    def pattern_to_bf16(u):
        raw = jnp.where(u >= 0x8000, u ^ 0x8000, u ^ 0xFFFF)
        return lax.bitcast_convert_type(lax.shift_left(raw, 16), F32).astype(BF16)

    def search(xs, kk, n_bits, to_value):
        def body(b, ts):
            bit = lax.shift_left(jnp.int32(1), n_bits - 1 - b)
            out = []
            for h in range(halves):
                cand = ts[h] | bit
                out.append(jnp.where(count(xs[h], to_value(cand), False) >= kk[h], cand, ts[h]))
            return tuple(out)

        ts = lax.fori_loop(0, n_bits, body, tuple(jnp.zeros((rh, LANES), I32) for _ in range(halves)),
                           unroll=SEARCH_UNROLL)
        return [to_value(t) for t in ts]

    hi_s, b1_s, b0_s = slabs(hi), slabs(b1), slabs(b0)
    k0 = [jnp.full((rh, LANES), k, F32) for _ in range(halves)]
    thr = search(hi_s, k0, 16, pattern_to_bf16)
    k1 = [k0[h] - count(hi_s[h], thr[h], True) for h in range(halves)]
    x1_s = [[jnp.where(hi_s[h][c] == thr[h], b1_s[h][c], minus1) for c in range(nslab)] for h in range(halves)]
    t1 = search(x1_s, k1, 8, int_to_bf16)
    k2 = [k1[h] - count(x1_s[h], t1[h], True) for h in range(halves)]
    x0_s = [[jnp.where(x1_s[h][c] == t1[h], b0_s[h][c], minus1) for c in range(nslab)] for h in range(halves)]
    t0 = search(x0_s, k2, 8, int_to_bf16)
    rows = []
    for h in range(halves):
        need = k2[h] - count(x0_s[h], t0[h], True)
        off = jnp.zeros((rh, LANES), F32)
        parts = []
        for c in range(nslab):
            above = (hi_s[h][c] > thr[h]) | (x1_s[h][c] > t1[h]) | (x0_s[h][c] > t0[h])
            eq = jnp.where(x0_s[h][c] == t0[h], one, zero)
            rank_ok = jnp.where(_dot(eq, ut) + off <= need, 1.0, 0.0).astype(BF16)
            parts.append(jnp.where(above, one, eq * rank_ok))
            off = off + _dot(eq, ones)
        rows.append(jnp.concatenate(parts, axis=1))
    return jnp.concatenate(rows, axis=0)


def _prompt_group_calls(kernel_fn, name, q_inputs, seq_inputs, const_inputs, b, s_len, qb):
    nq = s_len // qb
    outs = []
    for g in range(nq):
        n_keys = (g + 1) * qb
        in_specs = [pl.BlockSpec((None, qb, a.shape[2]), lambda bi, g=g: (bi, g, 0)) for a in q_inputs]
        in_specs += [pl.BlockSpec((None, n_keys, a.shape[2]), lambda bi: (bi, 0, 0)) for a in seq_inputs]
        in_specs += [_const_spec(a.shape) for a in const_inputs]
        args = list(q_inputs) + list(seq_inputs) + list(const_inputs)
        outs.append(pl.pallas_call(
            functools.partial(kernel_fn, g=g, qb=qb, n_keys=n_keys),
            grid=(b,),
            in_specs=in_specs,
            out_specs=pl.BlockSpec((None, qb, 1024), lambda bi: (bi, 0, 0)),
            out_shape=jax.ShapeDtypeStruct((b, qb, 1024), BF16),
            compiler_params=pltpu.CompilerParams(dimension_semantics=("parallel",), vmem_limit_bytes=VMEM_LIMIT),
            name=f"{name}_g{g}",
        )(*args))
    return outs


def _dsa_prompt_kernel(qi_ref, iw_ref, q_ref, kik_ref, kb_ref, vb_ref, ut_ref, o_ref, *, topk, g, qb, n_keys):
    qpos = g * qb + lax.broadcasted_iota(I32, (qb, 1), 0)
    kpos = lax.broadcasted_iota(I32, (1, n_keys), 1)
    valid = kpos <= qpos
    if n_keys <= topk:
        mask = valid
    else:
        iw = iw_ref[...]
        score = jnp.zeros((qb, n_keys), F32)
        for j in range(N_IDX_HEADS // 2):
            qs = qi_ref[:, j * LANES:(j + 1) * LANES]
            l0 = _dot_nt(qs, kik_ref[:, :LANES])
            l1 = _dot_nt(qs, kik_ref[:, LANES:])
            score = (score + jnp.maximum(l0, 0.0) * iw[:, 2 * j:2 * j + 1]
                     + jnp.maximum(l1, 0.0) * iw[:, 2 * j + 1:2 * j + 2])
        score = jnp.where(valid, score, NEG_FILL)
        mask = _topk_select(score, topk, ut_ref[...]).astype(F32) > 0.5
        if g * qb < topk:
            mask = mask & valid
    bias = jnp.where(mask, 0.0, NEG_INF)
    rep = N_HEADS_A // N_KV_A
    for kv in range(N_KV_A):
        kh = kb_ref[:, kv * LANES:(kv + 1) * LANES]
        vh = vb_ref[:, 2 * kv * LANES:(2 * kv + 2) * LANES]
        for r in range(rep):
            hh = kv * rep + r
            s = _dot_nt(q_ref[:, hh * LANES:(hh + 1) * LANES], kh) + bias
            m = jnp.max(s, axis=-1, keepdims=True)
            p = jnp.exp2(s - m).astype(BF16)
            pv = _dot(p, vh)
            o_ref[:, hh * LANES:(hh + 1) * LANES] = (pv[:, :LANES] / pv[:, LANES:]).astype(o_ref.dtype)


def _dsa_prompt(qi, iw, kik, q, kb, vb, ut, b, s_len, topk, qb):
    def v3(a):
        return a.reshape(b, s_len, a.shape[-1])

    return _prompt_group_calls(functools.partial(_dsa_prompt_kernel, topk=topk), "dsa_prompt",
                               [v3(qi), v3(iw), v3(q)], [v3(kik), v3(kb), v3(vb)], [ut], b, s_len, qb)


def _mla_prompt_kernel(qn_ref, qpe_ref, kn_ref, kpek_ref, vm_ref, o_ref, *, g, qb, n_keys):
    qpos = g * qb + lax.broadcasted_iota(I32, (qb, 1), 0)
    kpos = (n_keys - qb) + lax.broadcasted_iota(I32, (1, qb), 1)
    diag_ok = kpos <= qpos
    for h in range(N_HEADS_M):
        sl = slice(h * LANES, (h + 1) * LANES)
        qcat = jnp.concatenate([qn_ref[:, sl], qpe_ref[:, (h // 2) * LANES:(h // 2 + 1) * LANES]], axis=1)
        kcat = jnp.concatenate([kn_ref[:, sl], kpek_ref[:, (h % 2) * LANES:(h % 2 + 1) * LANES]], axis=1)
        s = _dot_nt(qcat, kcat)
        s_diag = jnp.where(diag_ok, s[:, n_keys - qb:], NEG_INF)
        s = s_diag if n_keys == qb else jnp.concatenate([s[:, :n_keys - qb], s_diag], axis=1)
        m = jnp.max(s, axis=-1, keepdims=True)
        p = jnp.exp2(s - m).astype(BF16)
        r = _dot(p, vm_ref[:, 2 * h * LANES:(2 * h + 2) * LANES])
        o_ref[:, sl] = (r[:, :LANES] / r[:, LANES:]).astype(o_ref.dtype)


def _mla_prompt(qn, qpe, kn, kpek, vm, b, s_len, qb):
    def v3(a):
        return a.reshape(b, s_len, a.shape[-1])

    return _prompt_group_calls(_mla_prompt_kernel, "mla_prompt", [v3(qn), v3(qpe)], [v3(kn), v3(kpek), v3(vm)], [],
                               b, s_len, qb)


def _merge_mlp_kernel(*refs, n_grp, ff_chunk):
    x_ref = refs[0]
    oa_refs = refs[1:1 + n_grp]
    ob_refs = refs[1 + n_grp:1 + 2 * n_grp]
    g_ref, wo_ref, g2_ref, wup_ref, wdn_ref, y_ref, m_ref = refs[1 + 2 * n_grp:]
    d = x_ref.shape[1]
    grp = pl.program_id(0) % n_grp

    def merge(j):
        ga = g_ref[:, :d].astype(F32)
        gb = g_ref[:, d:].astype(F32)
        m_ref[...] = (ga * oa_refs[j][...].astype(F32) + gb * ob_refs[j][...].astype(F32)).astype(BF16)

    for j in range(n_grp):
        pl.when(grp == j)(functools.partial(merge, j))
    x1 = x_ref[...] + _dot(m_ref[...], wo_ref[...])
    ms = jnp.mean(x1 * x1, axis=-1, keepdims=True)
    h2 = (x1 * lax.rsqrt(ms + NORM_EPS) * g2_ref[...]).astype(BF16)
    acc = x1
    for c in range(wup_ref.shape[1] // ff_chunk):
        u = jnp.maximum(_dot(h2, wup_ref[:, c * ff_chunk:(c + 1) * ff_chunk]), 0.0)
        acc = acc + _dot((u * u).astype(BF16), wdn_ref[c * ff_chunk:(c + 1) * ff_chunk, :])
    y_ref[...] = acc


def _merge_mlp(x2d, oa_list, ob_list, g, wo, g2, wup, wdn, tm):
    n, d = x2d.shape
    n_grp = len(oa_list)
    assert len(ob_list) == n_grp and all(a.shape[1:] == (tm, d) for a in oa_list + ob_list)

    def row(w):
        return pl.BlockSpec((tm, w), lambda i: (i, 0))

    grp_spec = pl.BlockSpec((None, tm, d), lambda i: (i // n_grp, 0, 0))
    return pl.pallas_call(
        functools.partial(_merge_mlp_kernel, n_grp=n_grp, ff_chunk=1024),
        grid=(n // tm,),
        in_specs=[row(d)] + [grp_spec] * (2 * n_grp) + [row(2 * d), _const_spec(wo.shape), _const_spec(g2.shape),
                                                        _const_spec(wup.shape), _const_spec(wdn.shape)],
        out_specs=row(d),
        out_shape=jax.ShapeDtypeStruct((n, d), F32),
        scratch_shapes=[pltpu.VMEM((tm, d), BF16)],
        compiler_params=pltpu.CompilerParams(dimension_semantics=("arbitrary",), vmem_limit_bytes=VMEM_LIMIT),
        name="merge_mlp",
    )(x2d, *oa_list, *ob_list, g, wo, g2, wup, wdn)


def _page_copies_t(pt_ref, cache_ref, buf_ref, sem_ref, seq, slot, n_pages, page):
    return [pltpu.make_async_copy(cache_ref.at[pt_ref[seq * n_pages + p]],
                                  buf_ref.at[slot, :, pl.ds(p * page, page)],
                                  sem_ref.at[slot]) for p in range(n_pages)]


def _samp_score_kernel(pt_ref, qi_ref, wc_ref, kin_ref, cache_ref, o_ref, buf_ref, sem_ref, *, n_pages, page, chunk):
    s = pl.program_id(0)
    ns = pl.num_programs(0)
    slot = s % 2
    past = n_pages * page

    @pl.when(s == 0)
    def _():
        for cp in _page_copies_t(pt_ref, cache_ref, buf_ref, sem_ref, 0, 0, n_pages, page):
            cp.start()

    @pl.when(s + 1 < ns)
    def _():
        for cp in _page_copies_t(pt_ref, cache_ref, buf_ref, sem_ref, s + 1, 1 - slot, n_pages, page):
            cp.start()

    pltpu.make_async_copy(buf_ref.at[1 - slot], buf_ref.at[slot], sem_ref.at[slot]).wait()

    qi = qi_ref[0]
    wc = wc_ref[0]

    def score_of(keys_t):
        l = _dot(qi, keys_t.astype(BF16))
        return jnp.sum(jnp.maximum(l, 0.0) * wc, axis=0, keepdims=True)

    for c in range(past // chunk):
        o_ref[0, :, c * chunk:(c + 1) * chunk] = score_of(buf_ref[slot, :, c * chunk:(c + 1) * chunk])
    lane = lax.broadcasted_iota(I32, (1, LANES), 1)
    tail_keys = jnp.where(lane == 0, kin_ref[0], 0.0)
    o_ref[0, :, past:] = jnp.where(lane == 0, score_of(tail_keys), NEG_FILL)


def _samp_scores(pt_flat, qi3, wc3, kin3, cache_idx_t, n_pages, page):
    ns = qi3.shape[0]
    past = n_pages * page
    chunk = min(2048, past)
    grid_spec = pltpu.PrefetchScalarGridSpec(
        num_scalar_prefetch=1,
        grid=(ns,),
        in_specs=[pl.BlockSpec((1, N_IDX_HEADS, IDX_DIM), lambda s, pt: (s, 0, 0)),
                  pl.BlockSpec((1, N_IDX_HEADS, 1), lambda s, pt: (s, 0, 0)),
                  pl.BlockSpec((1, IDX_DIM, 1), lambda s, pt: (s, 0, 0)),
                  pl.BlockSpec(memory_space=pl.ANY)],
        out_specs=pl.BlockSpec((1, 1, past + LANES), lambda s, pt: (s, 0, 0)),
        scratch_shapes=[pltpu.VMEM((2, IDX_DIM, past), F32), pltpu.SemaphoreType.DMA((2,))],
    )
    return pl.pallas_call(
        functools.partial(_samp_score_kernel, n_pages=n_pages, page=page, chunk=chunk),
        grid_spec=grid_spec,
        out_shape=jax.ShapeDtypeStruct((ns, 1, past + LANES), F32),
        compiler_params=pltpu.CompilerParams(dimension_semantics=("arbitrary",), vmem_limit_bytes=VMEM_LIMIT),
        name="sample_scores",
    )(pt_flat, qi3, wc3, kin3, cache_idx_t)


def _samp_select_kernel(score_ref, ut_ref, j8_ref, idx_ref, np_ref, rank_ref, *, topk, past, chunk):
    ns = score_ref.shape[0]
    sel = _topk_select(score_ref[...], topk, ut_ref[...])
    off = jnp.zeros((ns, 1), F32)
    for c in range(past // LANES):
        sl = slice(c * LANES, (c + 1) * LANES)
        sc = sel[:, sl]
        pc = _dot(sc, ut_ref[...])
        rank_ref[:, sl] = jnp.where(sc.astype(F32) > 0.5, pc + off, 0.0)
        off = off + pc[:, LANES - 1:LANES]
    np_ref[...] = jnp.broadcast_to(off, np_ref.shape).astype(I32)
    want = (lax.broadcasted_iota(I32, (topk, 1), 0) + 1).astype(F32)

    def per_seq(s, carry):
        acc = jnp.zeros((8, topk), F32)
        for c in range(past // chunk):
            rr = rank_ref[pl.ds(s, 1), c * chunk:(c + 1) * chunk]
            onehot = jnp.where(rr == want, 1.0, 0.0).astype(BF16)
            acc = acc + _dot_nt(j8_ref[:, c * chunk:(c + 1) * chunk], onehot)
        idx_ref[pl.ds(s, 1), :] = (acc[0:1, :] * float(LANES) + acc[1:2, :]).astype(I32)
        return carry

    lax.fori_loop(0, ns, per_seq, 0, unroll=2)


def _samp_select(score2d, ut, j8, topk, past):
    ns, n = score2d.shape
    chunk = min(1024, past)
    return pl.pallas_call(
        functools.partial(_samp_select_kernel, topk=topk, past=past, chunk=chunk),
        grid=(1,),
        in_specs=[pl.BlockSpec((ns, n), lambda i: (0, 0)), pl.BlockSpec(ut.shape, lambda i: (0, 0)),
                  pl.BlockSpec(j8.shape, lambda i: (0, 0))],
        out_specs=[pl.BlockSpec((ns, topk), lambda i: (0, 0)), pl.BlockSpec((ns, LANES), lambda i: (0, 0))],
        out_shape=[jax.ShapeDtypeStruct((ns, topk), I32), jax.ShapeDtypeStruct((ns, LANES), I32)],
        scratch_shapes=[pltpu.VMEM((ns, past), F32)],
        compiler_params=pltpu.CompilerParams(dimension_semantics=("arbitrary",), vmem_limit_bytes=VMEM_LIMIT),
        name="sample_select",
    )(score2d, ut, j8)


def _dsa_rows_attend(q_ref, kn_ref, vn_ref, kbuf, vbuf, o_ref, n_past, topk):
    slot_ok = lax.broadcasted_iota(I32, (1, topk), 1) < n_past
    new_ok = n_past < topk
    rep = N_HEADS_A // N_KV_A
    for kv in range(N_KV_A):
        kh = kbuf[:, kv, :].astype(BF16)
        vh = vbuf[:, kv, :].astype(BF16)
        q2 = q_ref[0, kv * rep:(kv + 1) * rep, :]
        k_new = kn_ref[0, kv:kv + 1, :].astype(BF16).astype(F32)
        v_new = vn_ref[0, kv:kv + 1, :].astype(BF16).astype(F32)
        sc = jnp.where(slot_ok, _dot_nt(q2, kh), NEG_INF)
        s_new = jnp.sum(q2.astype(F32) * k_new, axis=-1, keepdims=True)
        s_new = jnp.where(new_ok, s_new, NEG_INF)
        m = jnp.maximum(jnp.max(sc, axis=-1, keepdims=True), s_new)
        p = jnp.exp2(sc - m)
        p_new = jnp.exp2(s_new - m)
        l = jnp.sum(p, axis=-1, keepdims=True) + p_new
        o = (_dot(p.astype(BF16), vh) + p_new.astype(BF16).astype(F32) * v_new) / l
        for g in range(rep):
            hh = kv * rep + g
            o_ref[0, :, hh * LANES:(hh + 1) * LANES] = o[g:g + 1, :]


def _page_copies(pt_ref, cache_ref, buf_ref, sem_ref, seq, slot, n_pages, page):
    return [pltpu.make_async_copy(cache_ref.at[pt_ref[seq * n_pages + p]],
                                  buf_ref.at[slot, pl.ds(p * page, page)],
                                  sem_ref.at[slot]) for p in range(n_pages)]


def _samp_attn_kernel(pt_ref, idx_ref, np_ref, qn_ref, qcol_ref, gcol_ref, qpe_ref, cn_ref, knn_ref, kpn_ref,
                      wukt_ref, wuktf_ref, wuv_ref, qa_ref, kna_ref, vna_ref, cc_ref, cp_ref, ck_ref, cv_ref,
                      o_ref, oa_ref, cbuf, pbuf, s_ref, kbuf, vbuf, sem_ref, row_sem, *, n_pages, page, chunk, topk):
    s = pl.program_id(0)
    ns = pl.num_programs(0)
    slot = s % 2
    past = n_pages * page
    n_chunks = past // chunk
    rows_per_chunk = topk // n_chunks

    def copies(seq, sl):
        return (_page_copies(pt_ref, cc_ref, cbuf, sem_ref.at[0], seq, sl, n_pages, page)
                + _page_copies_t(pt_ref, cp_ref, pbuf, sem_ref.at[1], seq, sl, n_pages, page))

    @pl.when(s == 0)
    def _():
        for cp in copies(0, 0):
            cp.start(priority=PAGE_DMA_PRIORITY)

    @pl.when(s + 1 < ns)
    def _():
        for cp in copies(s + 1, 1 - slot):
            cp.start(priority=PAGE_DMA_PRIORITY)

    pltpu.make_async_copy(cbuf.at[1 - slot], cbuf.at[slot], sem_ref.at[0, slot]).wait()
    pltpu.make_async_copy(pbuf.at[1 - slot], pbuf.at[slot], sem_ref.at[1, slot]).wait()

    qg = qcol_ref[0].astype(F32) * gcol_ref[...]
    prod = wuktf_ref[...] * qg
    q_abs = jnp.concatenate([jnp.sum(prod[h * LANES:(h + 1) * LANES, :], axis=0, keepdims=True)
                             for h in range(N_HEADS_M)], axis=0).astype(BF16)
    qpe = qpe_ref[0]

    def score_chunk(c, carry):
        for r in range(rows_per_chunk):
            row = c * rows_per_chunk + r
            pos = idx_ref[s * topk + row]
            pg = pt_ref[s * n_pages + pos // page]
            off = pos % page
            pltpu.make_async_copy(ck_ref.at[pg, off], kbuf.at[row], row_sem.at[0]).start()
            pltpu.make_async_copy(cv_ref.at[pg, off], vbuf.at[row], row_sem.at[1]).start()
        start = pl.multiple_of(c * chunk, chunk)
        cb = cbuf[slot, pl.ds(start, chunk), :].astype(BF16)
        knt = _dot_nt(wukt_ref[...], cb)
        ssq = jnp.concatenate([jnp.sum(jnp.square(knt[h * LANES:(h + 1) * LANES, :]), axis=0, keepdims=True)
                               for h in range(N_HEADS_M)], axis=0)
        s_nope = _dot_nt(q_abs, cb) * lax.rsqrt(ssq * (1.0 / NOPE_DIM) + NORM_EPS)
        s_pe = _dot(qpe, pbuf[slot, :, pl.ds(start, chunk)].astype(BF16))
        s_ref[:, pl.ds(start, chunk)] = s_nope + s_pe
        return carry

    lax.fori_loop(0, n_chunks, score_chunk, 0)
    qn = qn_ref[0].astype(F32)
    s_new = (jnp.sum(qn * knn_ref[0].astype(F32), axis=-1, keepdims=True)
             + jnp.sum(qpe.astype(F32) * kpn_ref[0].astype(BF16).astype(F32), axis=-1, keepdims=True))
    lane = lax.broadcasted_iota(I32, (1, LANES), 1)
    s_ref[:, past:] = jnp.where(lane == 0, s_new, NEG_INF)

    sc = s_ref[...]
    m = jnp.max(sc, axis=-1, keepdims=True)
    p = jnp.exp2(sc - m)
    l = jnp.sum(p, axis=-1, keepdims=True)
    s_ref[...] = p
    p_new = jnp.exp2(s_new - m)

    def pv_chunk(c, acc):
        start = pl.multiple_of(c * chunk, chunk)
        cb = cbuf[slot, pl.ds(start, chunk), :].astype(BF16)
        return acc + _dot(s_ref[:, pl.ds(start, chunk)].astype(BF16), cb)

    o_lat = lax.fori_loop(0, n_chunks, pv_chunk, jnp.zeros((N_HEADS_M, KV_LORA), F32))
    o_lat = (o_lat + p_new.astype(BF16).astype(F32) * cn_ref[0].astype(BF16).astype(F32)) / l
    o8 = _dot(o_lat.astype(BF16), wuv_ref[...])
    for h in range(N_HEADS_M):
        o_ref[0, :, h * LANES:(h + 1) * LANES] = o8[h:h + 1, h * LANES:(h + 1) * LANES]

    pltpu.make_async_copy(vbuf, kbuf, row_sem.at[0]).wait()
    pltpu.make_async_copy(kbuf, vbuf, row_sem.at[1]).wait()
    _dsa_rows_attend(qa_ref, kna_ref, vna_ref, kbuf, vbuf, oa_ref, np_ref[s], topk)


def _samp_attn(pt_flat, idx_flat, n_past, qn3, qcol, gcol, qpe3, cn3, knn3, kpn3, wukt, wukt_f32, wuv,
               qa3, kna3, vna3, cache_ckv, cache_kpe_t, cache_k, cache_v, n_pages, page, topk):
    ns = qn3.shape[0]
    past = n_pages * page
    chunk = min(1024, past)
    assert topk % (past // chunk) == 0

    def per_seq(shape):
        return pl.BlockSpec((1,) + shape, lambda s, *_: (s, 0, 0))

    def const(shape):
        nd = len(shape)
        return pl.BlockSpec(shape, lambda s, *_: (0,) * nd, pipeline_mode=pl.Buffered(1))

    hbm = pl.BlockSpec(memory_space=pl.ANY)
    grid_spec = pltpu.PrefetchScalarGridSpec(
        num_scalar_prefetch=3,
        grid=(ns,),
        in_specs=[per_seq((N_HEADS_M, NOPE_DIM)), per_seq((N_HEADS_M * NOPE_DIM, 1)), const(gcol.shape),
                  per_seq((N_HEADS_M, ROPE_DIM_M)), per_seq((1, KV_LORA)), per_seq((N_HEADS_M, NOPE_DIM)),
                  per_seq((1, ROPE_DIM_M)), const(wukt.shape), const(wukt_f32.shape), const(wuv.shape),
                  per_seq((N_HEADS_A, HEAD_DIM_A)), per_seq((N_KV_A, HEAD_DIM_A)), per_seq((N_KV_A, HEAD_DIM_A)),
                  hbm, hbm, hbm, hbm],
        out_specs=[per_seq((1, N_HEADS_M * V_DIM_M)), per_seq((1, N_HEADS_A * HEAD_DIM_A))],
        scratch_shapes=[pltpu.VMEM((2, past, KV_LORA), F32), pltpu.VMEM((2, ROPE_DIM_M, past), F32),
                        pltpu.VMEM((N_HEADS_M, past + LANES), F32),
                        pltpu.VMEM((topk, N_KV_A, HEAD_DIM_A), F32), pltpu.VMEM((topk, N_KV_A, HEAD_DIM_A), F32),
                        pltpu.SemaphoreType.DMA((2, 2)), pltpu.SemaphoreType.DMA((2,))],
    )
    out_row = jax.ShapeDtypeStruct((ns, 1, N_HEADS_M * V_DIM_M), F32)
    return pl.pallas_call(
        functools.partial(_samp_attn_kernel, n_pages=n_pages, page=page, chunk=chunk, topk=topk),
        grid_spec=grid_spec,
        out_shape=[out_row, out_row],
        compiler_params=pltpu.CompilerParams(dimension_semantics=("arbitrary",), vmem_limit_bytes=VMEM_LIMIT),
        name="sample_attn",
    )(pt_flat, idx_flat, n_past, qn3, qcol, gcol, qpe3, cn3, knn3, kpn3, wukt, wukt_f32, wuv,
      qa3, kna3, vna3, cache_ckv, cache_kpe_t, cache_k, cache_v)


def _prep_weights(norm1_g, w_in, a_qn_g, a_kn_g, q_a_norm_g, w_q_b, kv_a_norm_g, w_uk, w_uv,
                  m_qn_g, m_kn_g, m_qr_g, m_kr_g):
    d = w_in.shape[0]
    sizes = (1024, 512, 512, 512, IDX_DIM, N_IDX_HEADS, Q_LORA, KV_LORA, ROPE_DIM_M, 2 * d)
    offs = [0]
    for sz in sizes:
        offs.append(offs[-1] + sz)
    wq, wk, wv, wqi, wki, wiw, wqa, wkva, wpe, wg = [w_in[:, offs[i]:offs[i + 1]] for i in range(len(sizes))]
    w_perm = jnp.concatenate(
        [wq, wk, wv, wqi, wki, wpe, wiw, jnp.zeros((d, LANES - N_IDX_HEADS), w_in.dtype), wqa, wkva, wg],
        axis=1).astype(BF16)
    wqb = jnp.concatenate([w_q_b[:, :, :NOPE_DIM].reshape(Q_LORA, -1), w_q_b[:, :, NOPE_DIM:].reshape(Q_LORA, -1)],
                          axis=1).astype(BF16)

    def rowv(v):
        return v.astype(F32).reshape(1, -1)

    return {
        "g1": rowv(norm1_g), "w_in": w_perm, "gq": rowv(a_qn_g), "gk": rowv(a_kn_g), "gqa": rowv(q_a_norm_g),
        "wqb": wqb, "gkva": rowv(kv_a_norm_g), "wuk": w_uk.reshape(KV_LORA, -1).astype(BF16),
        "wuv": w_uv.reshape(KV_LORA, -1).astype(BF16),
        "gqn": rowv(m_qn_g), "gkn": rowv(m_kn_g), "gqr": rowv(jnp.tile(m_qr_g, 2)),
        "gkp": rowv(jnp.concatenate([jnp.ones((IDX_DIM,), F32), m_kr_g.astype(F32)])),
    }


def kernel(x_prompt, x_sample, cache_k, cache_v, cache_idx_k, cache_ckv, cache_kpe, page_table,
           norm1_g, w_in, a_qn_g, a_kn_g, q_a_norm_g, w_q_b, kv_a_norm_g, w_uk, w_uv,
           m_qn_g, m_kn_g, m_qr_g, m_kr_g, w_o, norm2_g, w_up, w_down):
    b, s_len, d = x_prompt.shape
    ns, t_s, _ = x_sample.shape
    page = cache_k.shape[1]
    n_pages = page_table.shape[1]
    past = n_pages * page
    assert d == N_HEADS_A * HEAD_DIM_A and t_s == 1
    assert w_in.shape[1] == C_END - (LANES - N_IDX_HEADS)

    wts = _prep_weights(norm1_g, w_in, a_qn_g, a_kn_g, q_a_norm_g, w_q_b, kv_a_norm_g, w_uk, w_uv,
                        m_qn_g, m_kn_g, m_qr_g, m_kr_g)
    wo = w_o.astype(BF16)
    wup = w_up.astype(BF16)
    wdn = w_down.astype(BF16)
    g2 = norm2_g.astype(F32).reshape(1, -1)
    ut = (jnp.arange(LANES)[:, None] <= jnp.arange(LANES)[None, :]).astype(BF16)

    tm = min(256, s_len)
    qb = min(PROMPT_Q_BLOCK, s_len)
    tab_p = _rope_tables(jnp.arange(s_len))
    xp = x_prompt.reshape(b * s_len, d)
    (q, kf, kb, vf, vb, qi, kif, kpef, kik, kpek, iw, qn, qpe, cf, kn, vm, g) = _project(xp, tab_p, s_len // tm, tm, wts)
    topk_p = min(TOPK_MAX, s_len // 4)
    oa = _dsa_prompt(qi, iw, kik, q, kb, vb, ut, b, s_len, topk_p, qb)
    ob = _mla_prompt(qn, qpe, kn, kpek, vm, b, s_len, qb)
    y_prompt = _merge_mlp(xp, oa, ob, g, wo, g2, wup, wdn, qb).reshape(b, s_len, d)

    tab_s = _rope_tables(jnp.full((ns,), past, I32))
    xs = x_sample.reshape(ns, d)
    (sq, skf, _skb, svf, _svb, sqi, skif, skpef, _skik, _skpek, siw, sqn, sqpe, scf, skn, _svm, sg) = _project(
        xs, tab_s, 1, ns, wts)
    topk_s = min(TOPK_MAX, (past + 1) // 4)
    pt_flat = page_table.reshape(-1).astype(I32)
    cache_idx_t = jnp.swapaxes(cache_idx_k, 1, 2)
    cache_kpe_t = jnp.swapaxes(cache_kpe, 1, 2)
    score = _samp_scores(pt_flat, sqi.reshape(ns, N_IDX_HEADS, IDX_DIM), siw[:, :N_IDX_HEADS].reshape(ns, N_IDX_HEADS, 1),
                         skif.reshape(ns, IDX_DIM, 1), cache_idx_t, n_pages, page)
    pos = jnp.arange(past)
    j8 = jnp.zeros((8, past), F32).at[0].set(pos // LANES).at[1].set(pos % LANES).astype(BF16)
    idx, n_past = _samp_select(score.reshape(ns, past + LANES), ut, j8, topk_s, past)
    gcol = jnp.tile(m_kn_g.astype(F32), N_HEADS_M).reshape(-1, 1)
    wukt_f32 = w_uk.reshape(KV_LORA, -1).T.astype(F32)
    ob_s, oa_s = _samp_attn(pt_flat, idx.reshape(-1), n_past[:, 0],
                            sqn.reshape(ns, N_HEADS_M, NOPE_DIM), sqn.reshape(ns, N_HEADS_M * NOPE_DIM, 1), gcol,
                            sqpe.reshape(ns, N_HEADS_M, ROPE_DIM_M), scf.reshape(ns, 1, KV_LORA),
                            skn.reshape(ns, N_HEADS_M, NOPE_DIM), skpef.reshape(ns, 1, ROPE_DIM_M),
                            wukt_f32.astype(BF16), wukt_f32, wts["wuv"],
                            sq.reshape(ns, N_HEADS_A, HEAD_DIM_A), skf, svf,
                            cache_ckv, cache_kpe_t, cache_k, cache_v, n_pages, page, topk_s)
    y_sample = _merge_mlp(xs, [oa_s.reshape(1, ns, d)], [ob_s.reshape(1, ns, d)], sg, wo, g2, wup, wdn,
                          ns).reshape(ns, 1, d)

    return (y_prompt, y_sample,
            kf.reshape(b, s_len, N_KV_A, HEAD_DIM_A), vf.reshape(b, s_len, N_KV_A, HEAD_DIM_A),
            kif.reshape(b, s_len, IDX_DIM), cf.reshape(b, s_len, KV_LORA), kpef.reshape(b, s_len, ROPE_DIM_M),
            skf.reshape(ns, 1, N_KV_A, HEAD_DIM_A), svf.reshape(ns, 1, N_KV_A, HEAD_DIM_A),
            skif.reshape(ns, 1, IDX_DIM), scf.reshape(ns, 1, KV_LORA), skpef.reshape(ns, 1, ROPE_DIM_M))
```
